```python
import math
import jax, jax.numpy as jnp
from jax import lax
import numpy as np

D_MODEL = 1024
BATCH = 8
SEQ = 2048
DEPTH = 4
DEC_BATCH = 128
DEC_SEQ = 4
PAST_LEN = 2048
PAGE_SIZE = 128

HEAD_DIM = 64
N_BRANCH = 4
BRANCH_WIDTH = D_MODEL // N_BRANCH
SB_HEADS = BRANCH_WIDTH // HEAD_DIM
NSA_HEADS = BRANCH_WIDTH // HEAD_DIM
NSA_KV_HEADS = 2
NSA_GROUP = NSA_HEADS // NSA_KV_HEADS
NSA_KV_WIDTH = NSA_KV_HEADS * HEAD_DIM
CMP_BLOCK = 32
SEL_BLOCK = 64
N_SELECT = 16
WINDOW = 512
SC_CONV_LEN = 3
CC_CONV_LEN = 31
FFN_DIM = 2816
ROPE_THETA = 10000.0
NORM_EPS = 1e-6
QUERY_BLOCK = 128
SEL_QUERY_BLOCK = 64
NEG = -1e30
IN_SPLITS = (BRANCH_WIDTH,) * 4 + (NSA_KV_WIDTH,) * 6 + (3 * NSA_HEADS,) + (BRANCH_WIDTH,) * 5
W_IN_COLS = sum(IN_SPLITS)

kernel_name = "hybrid_sb_nsa_conv_decoder_step"


def _rmsnorm(x, g):
    xf = x.astype(jnp.float32)
    y = xf * lax.rsqrt(jnp.mean(xf * xf, axis=-1, keepdims=True) + NORM_EPS)
    return (y * g.astype(jnp.float32)).astype(x.dtype)


def _layernorm(x, g, b):
    xf = x.astype(jnp.float32)
    mu = jnp.mean(xf, axis=-1, keepdims=True)
    var = jnp.mean(jnp.square(xf - mu), axis=-1, keepdims=True)
    y = (xf - mu) * lax.rsqrt(var + NORM_EPS)
    return (y * g.astype(jnp.float32) + b.astype(jnp.float32)).astype(x.dtype)


def _rope(x, pos):
    half = HEAD_DIM // 2
    inv_freq = jnp.exp(-math.log(ROPE_THETA) * jnp.arange(half, dtype=jnp.float32) / half)
    ang = pos.astype(jnp.float32)[:, None] * inv_freq[None, :]
    cos = jnp.cos(ang)[None, :, None, :]
    sin = jnp.sin(ang)[None, :, None, :]
    xf = x.astype(jnp.float32)
    x1, x2 = xf[..., :half], xf[..., half:]
    return jnp.concatenate([x1 * cos - x2 * sin, x2 * cos + x1 * sin], axis=-1).astype(x.dtype)


def _swiglu(x, w_gu, w_down):
    g, u = jnp.split(x @ w_gu, 2, axis=-1)
    return (jax.nn.silu(g) * u) @ w_down


def _causal_dwconv(u, state, w):
    full = jnp.concatenate([state.astype(u.dtype), u], axis=1)
    out = lax.conv_general_dilated(full, w[:, None, :].astype(u.dtype), window_strides=(1,),
                                   padding="VALID", dimension_numbers=("NWC", "WIO", "NWC"),
                                   feature_group_count=u.shape[-1])
    return out, full[:, full.shape[1] - (w.shape[0] - 1):]


def _sweep_queries(fn, blk, q_pos, *qs):
    t = q_pos.shape[0]
    if t <= blk or t % blk:
        return fn(q_pos, *qs)
    nb = t // blk
    blocks = tuple(jnp.moveaxis(a.reshape(a.shape[0], nb, blk, *a.shape[2:]), 1, 0) for a in qs)
    out = lax.map(lambda args: fn(args[0], *args[1:]), (q_pos.reshape(nb, blk),) + blocks)
    return jnp.moveaxis(out, 0, 1).reshape(out.shape[1], t, *out.shape[3:])


def _stick_breaking(q_pos, q, k, v, k_pos):
    z = jnp.einsum("bqhd,bkhd->bhqk", q, k).astype(jnp.float32) * (HEAD_DIM ** -0.5)
    mask = (k_pos[None, :] < q_pos[:, None])[None, None]
    log_keep = jnp.where(mask, jax.nn.log_sigmoid(-z), 0.0)
    between = lax.cumsum(log_keep, axis=3, reverse=True) - log_keep
    w = jnp.where(mask, jnp.exp(jax.nn.log_sigmoid(z) + between), 0.0)
    return jnp.einsum("bhqk,bkhd->bqhd", w.astype(v.dtype), v)


def _compress(raw, pe, w):
    b, l, g, _ = raw.shape
    blocks = raw.reshape(b, l // CMP_BLOCK, CMP_BLOCK, g, HEAD_DIM) + pe[None, None, :, None, :]
    return jnp.einsum("bnlgd,lde->bnge", blocks, w)


def _nsa_cmp_sel(q_pos, qn, qr, kcmp, vcmp, ks_blk, vs_blk, n_sel):
    b, tq = qn.shape[:2]
    scale = HEAD_DIM ** -0.5
    qn5 = qn.reshape(b, tq, NSA_KV_HEADS, NSA_GROUP, HEAD_DIM)
    qr5 = qr.reshape(b, tq, NSA_KV_HEADS, NSA_GROUP, HEAD_DIM)
    nc, ns = kcmp.shape[1], ks_blk.shape[2]
    s = jnp.einsum("bqgjd,bcgd->bqgjc", qn5, kcmp).astype(jnp.float32) * scale
    cmask = ((jnp.arange(nc) + 1) * CMP_BLOCK - 1)[None, :] <= q_pos[:, None]
    any_valid = jnp.any(cmask, axis=-1)[None, :, None, None, None]
    p = jax.nn.softmax(jnp.where(cmask[None, :, None, None, :], s, NEG), axis=-1) * any_valid
    o_cmp = jnp.einsum("bqgjc,bcgd->bqgjd", p.astype(vcmp.dtype), vcmp)
    imp = p.sum(axis=3).reshape(b, tq, NSA_KV_HEADS, ns, SEL_BLOCK // CMP_BLOCK).sum(-1)
    blk = jnp.arange(ns)[None, :]
    cur = (q_pos // SEL_BLOCK)[:, None]
    forced = (blk == 0) | (blk == cur) | (blk == cur - 1)
    valid = blk * SEL_BLOCK <= q_pos[:, None]
    imp = jnp.where(forced[None, :, None, :], jnp.inf, jnp.where(valid[None, :, None, :], imp, -jnp.inf))
    _, idx = lax.top_k(imp, n_sel)
    idx_g = jnp.transpose(idx, (0, 2, 1, 3)).reshape(b, NSA_KV_HEADS, tq * n_sel)
    bi = jnp.arange(b)[:, None, None]
    gi = jnp.arange(NSA_KV_HEADS)[None, :, None]
    ksel = ks_blk[bi, gi, idx_g].reshape(b, NSA_KV_HEADS, tq, n_sel, SEL_BLOCK, HEAD_DIM)
    vsel = vs_blk[bi, gi, idx_g].reshape(b, NSA_KV_HEADS, tq, n_sel, SEL_BLOCK, HEAD_DIM)
    kpos = idx_g.reshape(b, NSA_KV_HEADS, tq, n_sel)[..., None] * SEL_BLOCK + jnp.arange(SEL_BLOCK)
    smask = (kpos <= q_pos[None, None, :, None, None])[:, :, :, None]
    ss = jnp.einsum("bqgjd,bgqnsd->bgqjns", qr5, ksel).astype(jnp.float32) * scale
    ss = jnp.where(smask, ss, NEG)
    ps = jax.nn.softmax(ss.reshape(*ss.shape[:4], n_sel * SEL_BLOCK), axis=-1).reshape(ss.shape)
    o_sel = jnp.einsum("bgqjns,bgqnsd->bqgjd", ps.astype(vsel.dtype), vsel)
    return jnp.stack([o_cmp, o_sel], axis=2)


def _window_attend(q_pos, q, k, v, k_pos):
    b, tq = q.shape[:2]
    q5 = q.reshape(b, tq, NSA_KV_HEADS, NSA_GROUP, HEAD_DIM)
    s = jnp.einsum("bqgjd,bkgd->bqgjk", q5, k).astype(jnp.float32) * (HEAD_DIM ** -0.5)
    kp, qp = k_pos[None, :], q_pos[:, None]
    mask = (kp <= qp) & (kp > qp - WINDOW) & (kp >= 0)
    p = jax.nn.softmax(jnp.where(mask[None, :, None, None, :], s, NEG), axis=-1)
    return jnp.einsum("bqgjk,bkgd->bqgjd", p.astype(v.dtype), v)


def _token_mixing(h, lp, pos0, past):
    b, t, _ = h.shape
    dt = h.dtype
    pos = pos0 + jnp.arange(t, dtype=jnp.int32)
    cuts = np.cumsum(IN_SPLITS)[:-1].tolist()
    (sb_q, sb_k, sb_v, nq, ck, cv, sk, sv, wk, wv, ngate,
     sc_b, sc_c, sc_h, cc_a, cc_g) = jnp.split(h @ lp["w_in"], cuts, axis=-1)
    heads = lambda a, n: a.reshape(b, t, n, HEAD_DIM)

    sb_q, sb_k, sb_v = heads(sb_q, SB_HEADS), heads(sb_k, SB_HEADS), heads(sb_v, SB_HEADS)
    new_sb = jnp.stack([sb_k, sb_v], axis=2)
    if past is None:
        ka, va, kpos_a = sb_k, sb_v, pos
    else:
        ka = jnp.concatenate([past["sb"][:, :, 0], sb_k], axis=1)
        va = jnp.concatenate([past["sb"][:, :, 1], sb_v], axis=1)
        kpos_a = jnp.arange(ka.shape[1], dtype=jnp.int32)
    y_a = _sweep_queries(lambda qp, qb: _stick_breaking(qp, qb, ka, va, kpos_a), QUERY_BLOCK, pos, sb_q)
    y_a = y_a.reshape(b, t, BRANCH_WIDTH)

    qn = _rmsnorm(heads(nq, NSA_HEADS), lp["q_gain"])
    qr = _rope(qn, pos)
    new_cmp = jnp.stack([heads(ck, NSA_KV_HEADS), heads(cv, NSA_KV_HEADS)], axis=2)
    ks = _rope(_rmsnorm(heads(sk, NSA_KV_HEADS), lp["k_gain"]), pos)
    new_sel = jnp.stack([ks, heads(sv, NSA_KV_HEADS)], axis=2)
    kw = _rope(_rmsnorm(heads(wk, NSA_KV_HEADS), lp["k_gain"]), pos)
    win_new = jnp.stack([kw, heads(wv, NSA_KV_HEADS)], axis=2)
    if past is None:
        cmp_all, sel_all, win_all = new_cmp, new_sel, win_new
    else:
        cmp_all = jnp.concatenate([past["cmp"], new_cmp], axis=1)
        sel_all = jnp.concatenate([past["sel"], new_sel], axis=1)
        win_all = jnp.concatenate([past["win"], win_new], axis=1)
    l_all = cmp_all.shape[1]
    l_pad = -(-l_all // SEL_BLOCK) * SEL_BLOCK
    padw = ((0, 0), (0, l_pad - l_all), (0, 0), (0, 0), (0, 0))
    cmp_all, sel_all = jnp.pad(cmp_all, padw), jnp.pad(sel_all, padw)
    kcmp = _rmsnorm(_compress(cmp_all[:, :, 0], lp["cmp_pe"][0], lp["cmp_w"][0]), lp["k_gain"])
    vcmp = _compress(cmp_all[:, :, 1], lp["cmp_pe"][1], lp["cmp_w"][1])
    ns = l_pad // SEL_BLOCK
    sel_blk = jnp.transpose(sel_all.reshape(b, ns, SEL_BLOCK, 2, NSA_KV_HEADS, HEAD_DIM), (3, 0, 4, 1, 2, 5))
    n_sel = min(N_SELECT, ns)
    o_cs = _sweep_queries(lambda qp, a, c: _nsa_cmp_sel(qp, a, c, kcmp, vcmp, sel_blk[0], sel_blk[1], n_sel),
                          SEL_QUERY_BLOCK, pos, qn, qr)
    if past is None:
        win_pad = jnp.pad(win_all, ((0, 0), (WINDOW, 0), (0, 0), (0, 0), (0, 0)))

        def win_block(qp, qb):
            n = WINDOW + qp.shape[0]
            kv = lax.dynamic_slice_in_dim(win_pad, qp[0] - pos0, n, axis=1)
            return _window_attend(qp, qb, kv[:, :, 0], kv[:, :, 1], qp[0] - WINDOW + jnp.arange(n, dtype=jnp.int32))

        o_w = _sweep_queries(win_block, QUERY_BLOCK, pos, qr)
        new_win = win_all[:, t - min(WINDOW, t):]
    else:
        wb = past["win"].shape[1]
        kpos_w = pos0 - wb + jnp.arange(wb + t, dtype=jnp.int32)
        o_w = _window_attend(pos, qr, win_all[:, :, 0], win_all[:, :, 1], kpos_w)
        new_win = win_all[:, t:]
    g = jax.nn.sigmoid(ngate.reshape(b, t, NSA_KV_HEADS, NSA_GROUP, 3))
    y_b = (g[..., 0:1] * o_cs[:, :, 0] + g[..., 1:2] * o_cs[:, :, 1] + g[..., 2:3] * o_w).reshape(b, t, BRANCH_WIDTH)

    u = sc_c * sc_h
    st_c = past["sc"] if past is not None else jnp.zeros((b, SC_CONV_LEN - 1, BRANCH_WIDTH), dt)
    conv_c, new_sc = _causal_dwconv(u, st_c, lp["sc_w"])
    y_c = sc_b * conv_c

    a = cc_a * jax.nn.sigmoid(cc_g)
    st_d = past["cc"] if past is not None else jnp.zeros((b, CC_CONV_LEN - 1, BRANCH_WIDTH), dt)
    conv_d, new_cc = _causal_dwconv(a, st_d, lp["cc_w"])
    y_d = jax.nn.silu(_layernorm(conv_d + lp["cc_b"], lp["cc_g"], lp["cc_beta"]))

    ys = jnp.stack([y_a, y_b, y_c, y_d], axis=2)
    branches = jnp.einsum("btnc,ncd->btnd", ys, lp["w_branch"])
    gates = jax.nn.sigmoid(h @ lp["w_gate"] + lp["b_gate"]).reshape(b, t, N_BRANCH, D_MODEL)
    out = jnp.sum(gates * branches, axis=2) @ lp["w_out"]
    return out, (new_sb, new_cmp, new_sel, new_win, new_sc, new_cc)


def _layer(x, lp, pos0, past):
    x = x + 0.5 * _swiglu(_rmsnorm(x, lp["n1"]), lp["ff1_gu"], lp["ff1_down"])
    mix, states = _token_mixing(_rmsnorm(x, lp["nm"]), lp, pos0, past)
    x = x + mix
    x = x + 0.5 * _swiglu(_rmsnorm(x, lp["n2"]), lp["ff2_gu"], lp["ff2_down"])
    return x, states


def setup_inputs(seed: int = 0) -> dict:
    key = jax.random.key(seed)
    ks = jax.random.split(key, 32)
    nrm = lambda i, shape, scale: scale * jax.random.normal(ks[i], shape, jnp.float32)
    gain = lambda i, shape: 1.0 + 0.02 * jax.random.normal(ks[i], shape, jnp.float32)
    n_pages = PAST_LEN // PAGE_SIZE
    used = DEC_BATCH * n_pages
    n_phys = used + max(1, used // 4)
    win_buf = min(WINDOW, PAST_LEN)
    page_table = jax.random.permutation(ks[8], n_phys)[:used].reshape(DEC_BATCH, n_pages).astype(jnp.int32)
    return {
        "x_prompt": nrm(0, (BATCH, SEQ, D_MODEL), 1.0),
        "x_sample": nrm(1, (DEC_BATCH, DEC_SEQ, D_MODEL), 1.0),
        "cache_sb_kv": nrm(2, (DEPTH, n_phys, PAGE_SIZE, 2, SB_HEADS, HEAD_DIM), 1.0),
        "cache_cmp_kv": nrm(3, (DEPTH, n_phys, PAGE_SIZE, 2, NSA_KV_HEADS, HEAD_DIM), 1.0),
        "cache_sel_kv": nrm(4, (DEPTH, n_phys, PAGE_SIZE, 2, NSA_KV_HEADS, HEAD_DIM), 1.0),
        "state_win_kv": nrm(5, (DEPTH, DEC_BATCH, win_buf, 2, NSA_KV_HEADS, HEAD_DIM), 1.0),
        "state_conv_sc": nrm(6, (DEPTH, DEC_BATCH, SC_CONV_LEN - 1, BRANCH_WIDTH), 1.0),
        "state_conv_cc": nrm(7, (DEPTH, DEC_BATCH, CC_CONV_LEN - 1, BRANCH_WIDTH), 1.0),
        "page_table": page_table,
        "norm_ffn1": gain(9, (DEPTH, D_MODEL)),
        "w_ffn1_gu": nrm(10, (DEPTH, D_MODEL, 2 * FFN_DIM), D_MODEL ** -0.5),
        "w_ffn1_down": nrm(11, (DEPTH, FFN_DIM, D_MODEL), FFN_DIM ** -0.5),
        "norm_mix": gain(12, (DEPTH, D_MODEL)),
        "w_in": nrm(13, (DEPTH, D_MODEL, W_IN_COLS), D_MODEL ** -0.5),
        "nsa_q_gain": gain(14, (DEPTH, HEAD_DIM)),
        "nsa_k_gain": gain(15, (DEPTH, HEAD_DIM)),
        "nsa_cmp_pe": nrm(16, (DEPTH, 2, CMP_BLOCK, HEAD_DIM), 0.1),
        "nsa_cmp_w": nrm(17, (DEPTH, 2, CMP_BLOCK, HEAD_DIM, HEAD_DIM), (CMP_BLOCK * HEAD_DIM) ** -0.5),
        "sc_conv_w": nrm(18, (DEPTH, SC_CONV_LEN, BRANCH_WIDTH), SC_CONV_LEN ** -0.5),
        "cc_conv_w": nrm(19, (DEPTH, CC_CONV_LEN, BRANCH_WIDTH), CC_CONV_LEN ** -0.5),
        "cc_conv_b": nrm(20, (DEPTH, BRANCH_WIDTH), 0.02),
        "cc_norm_g": gain(21, (DEPTH, BRANCH_WIDTH)),
        "cc_norm_b": nrm(22, (DEPTH, BRANCH_WIDTH), 0.02),
        "w_branch": nrm(23, (DEPTH, N_BRANCH, BRANCH_WIDTH, D_MODEL), BRANCH_WIDTH ** -0.5),
        "w_gate": nrm(24, (DEPTH, D_MODEL, N_BRANCH * D_MODEL), D_MODEL ** -0.5),
        "b_gate": nrm(25, (DEPTH, N_BRANCH * D_MODEL), 0.02),
        "w_out": nrm(26, (DEPTH, D_MODEL, D_MODEL), D_MODEL ** -0.5),
        "norm_ffn2": gain(27, (DEPTH, D_MODEL)),
        "w_ffn2_gu": nrm(28, (DEPTH, D_MODEL, 2 * FFN_DIM), D_MODEL ** -0.5),
        "w_ffn2_down": nrm(29, (DEPTH, FFN_DIM, D_MODEL), FFN_DIM ** -0.5),
    }


def reference(x_prompt, x_sample, cache_sb_kv, cache_cmp_kv, cache_sel_kv, state_win_kv, state_conv_sc,
              state_conv_cc, page_table, norm_ffn1, w_ffn1_gu, w_ffn1_down, norm_mix, w_in, nsa_q_gain,
              nsa_k_gain, nsa_cmp_pe, nsa_cmp_w, sc_conv_w, cc_conv_w, cc_conv_b, cc_norm_g, cc_norm_b,
              w_branch, w_gate, b_gate, w_out, norm_ffn2, w_ffn2_gu, w_ffn2_down):
    n_seq = page_table.shape[0]
    past_len = page_table.shape[1] * cache_sb_kv.shape[2]

    def gather(cache, l):
        rows = cache[l, page_table]
        return rows.reshape(n_seq, past_len, *rows.shape[3:])

    xp, xs = x_prompt, x_sample
    new_p = [[] for _ in range(6)]
    new_s = [[] for _ in range(6)]
    for l in range(DEPTH):
        lp = {
            "n1": norm_ffn1[l], "ff1_gu": w_ffn1_gu[l], "ff1_down": w_ffn1_down[l],
            "nm": norm_mix[l], "w_in": w_in[l], "q_gain": nsa_q_gain[l], "k_gain": nsa_k_gain[l],
            "cmp_pe": nsa_cmp_pe[l], "cmp_w": nsa_cmp_w[l], "sc_w": sc_conv_w[l], "cc_w": cc_conv_w[l],
            "cc_b": cc_conv_b[l], "cc_g": cc_norm_g[l], "cc_beta": cc_norm_b[l], "w_branch": w_branch[l],
            "w_gate": w_gate[l], "b_gate": b_gate[l], "w_out": w_out[l],
            "n2": norm_ffn2[l], "ff2_gu": w_ffn2_gu[l], "ff2_down": w_ffn2_down[l],
        }
        past = {
            "sb": gather(cache_sb_kv, l), "cmp": gather(cache_cmp_kv, l), "sel": gather(cache_sel_kv, l),
            "win": state_win_kv[l], "sc": state_conv_sc[l], "cc": state_conv_cc[l],
        }
        xp, st_p = _layer(xp, lp, 0, None)
        xs, st_s = _layer(xs, lp, past_len, past)
        for i in range(6):
            new_p[i].append(st_p[i])
            new_s[i].append(st_s[i])
    sb_kv_prompt, sb_kv_sample = jnp.stack(new_p[0]), jnp.stack(new_s[0])
    cmp_kv_prompt, cmp_kv_sample = jnp.stack(new_p[1]), jnp.stack(new_s[1])
    sel_kv_prompt, sel_kv_sample = jnp.stack(new_p[2]), jnp.stack(new_s[2])
    win_kv_prompt, win_kv_sample = jnp.stack(new_p[3]), jnp.stack(new_s[3])
    conv_sc_prompt, conv_sc_sample = jnp.stack(new_p[4]), jnp.stack(new_s[4])
    conv_cc_prompt, conv_cc_sample = jnp.stack(new_p[5]), jnp.stack(new_s[5])
    return (xp, xs, sb_kv_prompt, sb_kv_sample, cmp_kv_prompt, cmp_kv_sample, sel_kv_prompt, sel_kv_sample,
            win_kv_prompt, win_kv_sample, conv_sc_prompt, conv_sc_sample, conv_cc_prompt, conv_cc_sample)
```

```python
import functools
import math

import numpy as np
import jax
import jax.numpy as jnp
from jax import lax
from jax.experimental import pallas as pl
from jax.experimental.pallas import tpu as pltpu

F32, BF16 = jnp.float32, jnp.bfloat16

HEAD_DIM = 64
BRANCH_WIDTH = 256
N_BRANCH = 4
NSA_KV_WIDTH = 128
CMP_BLOCK = 32
SEL_BLOCK = 64
N_SELECT = 16
WINDOW = 512
SC_CONV_LEN = 3
CC_CONV_LEN = 31
ROPE_THETA = 10000.0
NORM_EPS = 1e-6
NEG = -1e30
SCALE = HEAD_DIM ** -0.5

LANES = 128
SUBLANES = 8
VMEM_LIMIT_BYTES = 56 * 1024 * 1024
TOKEN_TILE = 512
FFN_CHUNK = 256
ATT_TQ = 128
SB_TK = 256
CONV_TC = 256
CONV_SUB = 64
CMP_ROW = CMP_BLOCK * 2 * NSA_KV_WIDTH

_NT = (((1,), (1,)), ((), ()))


def _params(*sem):
    return pltpu.CompilerParams(dimension_semantics=sem, vmem_limit_bytes=VMEM_LIMIT_BYTES)


def _full(shape):
    n = len(shape)
    return pl.BlockSpec(shape, lambda *_, n=n: (0,) * n)


def _dot(a, b):
    return jnp.dot(a, b, preferred_element_type=F32)


def _dot_nt(a, b):
    return lax.dot_general(a, b, _NT, preferred_element_type=F32)


def _div(x, n):
    assert n & (n - 1) == 0
    return x >> (n.bit_length() - 1)


def _mod(x, n):
    assert n & (n - 1) == 0
    return x & (n - 1)


def _rms(x, g):
    return x * lax.rsqrt(jnp.mean(x * x, axis=-1, keepdims=True) + NORM_EPS) * g


def _split_dot(x, m):
    hi = x.astype(BF16)
    lo = (x - hi.astype(F32)).astype(BF16)
    return _dot(hi, m) + _dot(lo, m)


def _head_rms(x, bd, g):
    ms = _split_dot(x * x, bd) * (1.0 / HEAD_DIM)
    return x * lax.rsqrt(ms + NORM_EPS) * g


def _rope(x, cos, sin_signed):
    w = x.shape[-1]
    lane = lax.broadcasted_iota(jnp.int32, x.shape, 1)
    first_half = (lane & (HEAD_DIM - 1)) < HEAD_DIM // 2
    rot = jnp.where(first_half, pltpu.roll(x, w - HEAD_DIM // 2, 1), pltpu.roll(x, HEAD_DIM // 2, 1))
    return x * cos + rot * sin_signed


def _ffn_kernel(x_ref, g_ref, wgu_ref, wd_ref, o_ref, h_ref, acc_ref):
    h_ref[...] = _rms(x_ref[...], g_ref[...]).astype(BF16)
    acc_ref[...] = jnp.zeros_like(acc_ref)

    def body(c, carry):
        h = h_ref[...]
        g = _dot(h, wgu_ref[0, c])
        u = _dot(h, wgu_ref[1, c])
        a = (g * jax.nn.sigmoid(g) * u).astype(BF16)
        acc_ref[...] += _dot(a, wd_ref[c])
        return carry

    lax.fori_loop(0, wd_ref.shape[0], body, 0)
    o_ref[...] = x_ref[...] + 0.5 * acc_ref[...]


def _ffn(x, g, wgu, wd):
    n, d = x.shape
    tm = min(TOKEN_TILE, n)
    return pl.pallas_call(
        _ffn_kernel,
        grid=(n // tm,),
        in_specs=[pl.BlockSpec((tm, d), lambda i: (i, 0)), _full(g.shape), _full(wgu.shape), _full(wd.shape)],
        out_specs=pl.BlockSpec((tm, d), lambda i: (i, 0)),
        out_shape=jax.ShapeDtypeStruct((n, d), F32),
        scratch_shapes=[pltpu.VMEM((tm, d), BF16), pltpu.VMEM((tm, d), F32)],
        compiler_params=_params("parallel"),
        name="ffn",
    )(x, g, wgu, wd)


_P_SBQ, _P_SBKV, _P_NQ, _P_CMP, _P_SEL, _P_WIN, _P_GATE = 0, 256, 768, 1024, 1280, 1536, 1792
_P_SCB, _P_SCC, _P_SCH, _P_CCA, _P_CCG, _P_END = 2560, 2816, 3072, 3328, 3584, 3840


def _proj_kernel(x_ref, g_ref, w_ref, cos_ref, sin_ref, qg_ref, kg_ref, bd_ref,
                 sbq_ref, sbkv_ref, qn_ref, qr_ref, cmp_ref, sel_ref, win_ref, gate_ref,
                 scb_ref, scu_ref, cca_ref):
    h = _rms(x_ref[...], g_ref[...]).astype(BF16)

    def mm(a, b):
        return _dot(h, w_ref[:, a:b])

    cos, sin, bd = cos_ref[...], sin_ref[...], bd_ref[...]
    sbq_ref[...] = mm(_P_SBQ, _P_SBKV)
    sbkv_ref[...] = mm(_P_SBKV, _P_NQ)
    qn = _head_rms(mm(_P_NQ, _P_CMP), bd, qg_ref[...])
    qn_ref[...] = qn
    qr_ref[...] = _rope(qn, cos, sin)
    cmp_ref[...] = mm(_P_CMP, _P_SEL)
    bd1, cos1, sin1 = bd[:LANES, :LANES], cos[:, :LANES], sin[:, :LANES]
    for a, ref in ((_P_SEL, sel_ref), (_P_WIN, win_ref)):
        kv = mm(a, a + 2 * NSA_KV_WIDTH)
        ref[:, :NSA_KV_WIDTH] = _rope(_head_rms(kv[:, :NSA_KV_WIDTH], bd1, kg_ref[...]), cos1, sin1)
        ref[:, NSA_KV_WIDTH:] = kv[:, NSA_KV_WIDTH:]
    gate_ref[...] = jax.nn.sigmoid(mm(_P_GATE, _P_SCB))
    scb_ref[...] = mm(_P_SCB, _P_SCC)
    scu_ref[...] = mm(_P_SCC, _P_SCH) * mm(_P_SCH, _P_CCA)
    cca_ref[...] = mm(_P_CCA, _P_CCG) * jax.nn.sigmoid(mm(_P_CCG, _P_END))


def _proj(x, g, w, cos, sin, qg, kg, bd):
    n, d = x.shape
    tm = min(TOKEN_TILE, n)
    n_rope_tiles = cos.shape[0] // tm
    widths = (256, 512, 256, 256, 256, 256, 256, 768, 256, 256, 256)
    tok = lambda wd_: pl.BlockSpec((tm, wd_), lambda i: (i, 0))
    rope = pl.BlockSpec((tm, cos.shape[1]), lambda i: (i % n_rope_tiles, 0))
    return pl.pallas_call(
        _proj_kernel,
        grid=(n // tm,),
        in_specs=[tok(d), _full(g.shape), _full(w.shape), rope, rope, _full(qg.shape), _full(kg.shape),
                  _full(bd.shape)],
        out_specs=[tok(wd_) for wd_ in widths],
        out_shape=[jax.ShapeDtypeStruct((n, wd_), F32) for wd_ in widths],
        compiler_params=_params("parallel"),
        name="proj",
    )(x, g, w, cos, sin, qg, kg, bd)


def _sb_tile(qm, k, v, valid, c, tri):
    z = _dot_nt(qm, k)
    sp = jnp.log1p(jnp.exp(-jnp.abs(z)))
    log_beta = jnp.minimum(z, 0.0) - sp
    log_keep = -jnp.maximum(z, 0.0) - sp
    if valid is not None:
        log_keep = jnp.where(valid, log_keep, 0.0)
    between = _split_dot(log_keep, tri) + c
    w = jnp.exp(log_beta + between)
    if valid is not None:
        w = jnp.where(valid, w, 0.0)
    return _dot(w.astype(BF16), v), c + jnp.sum(log_keep, axis=-1, keepdims=True)


def _sb_prompt_kernel(q_ref, kv_ref, tri_ref, o_ref, kvb_ref, acc_ref, *, tq, tk):
    i = pl.program_id(1)

    @pl.when(i == 0)
    def _():
        kvb_ref[...] = kv_ref[0].astype(BF16)

    t0 = i * tq
    lane = lax.broadcasted_iota(jnp.int32, (1, BRANCH_WIDTH), 1)
    head_mask = [((lane >> 6) == h).astype(F32) for h in range(4)]
    q = q_ref[0] * SCALE
    qm = jnp.concatenate([q * head_mask[h] for h in range(4)], axis=0).astype(BF16)
    trow = lax.broadcasted_iota(jnp.int32, (tq, 1), 0)
    tvec = t0 + jnp.concatenate([trow] * 4, axis=0)
    tri = tri_ref[...]
    acc_ref[...] = jnp.zeros_like(acc_ref)
    n_tiles = (t0 + tq - 1) // tk + 1

    def body(it, c):
        off = pl.multiple_of((n_tiles - 1 - it) * tk, tk)
        kv = kvb_ref[pl.ds(off, tk), :]
        kpos = off + lax.broadcasted_iota(jnp.int32, (1, tk), 1)
        pv, c = _sb_tile(qm, kv[:, :BRANCH_WIDTH], kv[:, BRANCH_WIDTH:], kpos < tvec, c, tri)
        acc_ref[...] += pv
        return c

    lax.fori_loop(0, n_tiles, body, jnp.zeros((4 * tq, 1), F32))
    acc = acc_ref[...]
    o_ref[0] = sum(acc[h * tq:(h + 1) * tq] * head_mask[h] for h in range(4))


def _tri(n):
    r = np.arange(n)
    return jnp.asarray(r[:, None] > r[None, :], BF16)


def _sb_prompt(q, kv):
    b, t, _ = q.shape
    tq, tk = min(ATT_TQ, t), min(SB_TK, t)
    return pl.pallas_call(
        functools.partial(_sb_prompt_kernel, tq=tq, tk=tk),
        grid=(b, t // tq),
        in_specs=[pl.BlockSpec((1, tq, BRANCH_WIDTH), lambda bi, i: (bi, i, 0)),
                  pl.BlockSpec((1, t, 2 * BRANCH_WIDTH), lambda bi, i: (bi, 0, 0)),
                  _full((tk, tk))],
        out_specs=pl.BlockSpec((1, tq, BRANCH_WIDTH), lambda bi, i: (bi, i, 0)),
        out_shape=jax.ShapeDtypeStruct((b, t, BRANCH_WIDTH), F32),
        scratch_shapes=[pltpu.VMEM((t, 2 * BRANCH_WIDTH), BF16), pltpu.VMEM((4 * tq, BRANCH_WIDTH), F32)],
        compiler_params=_params("parallel", "arbitrary"),
        name="sb_prompt",
    )(q, kv, _tri(tk))


def _sb_sample_kernel(pt_ref, q_ref, new_ref, tri_ref, *rest, n_pages, nd):
    pages, (o_ref, newt_ref) = rest[:n_pages], rest[n_pages:]

    @pl.when(pl.program_id(0) == 0)
    def _():
        newt_ref[...] = jnp.zeros_like(newt_ref)

    newt_ref[0:nd, :] = new_ref[0]
    rows = 4 * nd
    row = lax.broadcasted_iota(jnp.int32, (rows, 1), 0)
    lane = lax.broadcasted_iota(jnp.int32, (1, BRANCH_WIDTH), 1)
    head_mask = ((lane >> 6) == _div(row, nd)).astype(F32)
    qm = (q_ref[0] * SCALE * head_mask).astype(BF16)
    tri = tri_ref[...]
    newt = newt_ref[...].astype(BF16)
    key = lax.broadcasted_iota(jnp.int32, (1, newt.shape[0]), 1)
    acc, c = _sb_tile(qm, newt[:, :BRANCH_WIDTH], newt[:, BRANCH_WIDTH:], key < _mod(row, nd),
                      jnp.zeros((rows, 1), F32), tri)
    for p in reversed(range(n_pages)):
        pg = pages[p][...].astype(BF16)
        pv, c = _sb_tile(qm, pg[:, :BRANCH_WIDTH], pg[:, BRANCH_WIDTH:], None, c, tri)
        acc = acc + pv
    m = acc * head_mask
    s = m[0:8] + m[8:16]
    o_ref[0] = s + pltpu.roll(s, 4, 0)


def _sb_sample(page_table, q16, new, cache, layer):
    ns, n_pages = page_table.shape
    nd = new.shape[1]
    assert nd == 4 and q16.shape[1] == 16
    page = cache.shape[2]
    width = 2 * BRANCH_WIDTH
    page_specs = [pl.BlockSpec((None, None, page, width), lambda b, pt, p=p: (layer, pt[b, p], 0, 0))
                  for p in range(n_pages)]
    grid_spec = pltpu.PrefetchScalarGridSpec(
        num_scalar_prefetch=1,
        grid=(ns,),
        in_specs=[pl.BlockSpec((1, 16, BRANCH_WIDTH), lambda b, pt: (b, 0, 0)),
                  pl.BlockSpec((1, nd, width), lambda b, pt: (b, 0, 0)),
                  pl.BlockSpec((page, page), lambda b, pt: (0, 0))] + page_specs,
        out_specs=pl.BlockSpec((1, 8, BRANCH_WIDTH), lambda b, pt: (b, 0, 0)),
        scratch_shapes=[pltpu.VMEM((page, width), F32)],
    )
    out = pl.pallas_call(
        functools.partial(_sb_sample_kernel, n_pages=n_pages, nd=nd),
        grid_spec=grid_spec,
        out_shape=jax.ShapeDtypeStruct((ns, 8, BRANCH_WIDTH), F32),
        compiler_params=_params("arbitrary"),
        name="sb_sample",
    )(page_table, q16, new, _tri(page), *([cache] * n_pages))
    return out[:, :nd]


def _compress_math(x_ref, pe_ref, w_ref, kg, bd1):
    m = x_ref.shape[0]
    chunk = 1024
    acc = jnp.zeros((m, 2 * NSA_KV_WIDTH), F32)
    for c in range(CMP_ROW // chunk):
        sl = slice(c * chunk, (c + 1) * chunk)
        acc = acc + _dot((x_ref[:, sl] + pe_ref[:, sl]).astype(BF16), w_ref[sl, :])
    return _head_rms(acc[:, :NSA_KV_WIDTH], bd1, kg), acc[:, NSA_KV_WIDTH:]


def _compress_prompt_kernel(x_ref, pe_ref, w_ref, kg_ref, bd_ref, o_ref):
    k, v = _compress_math(x_ref.at[0], pe_ref, w_ref, kg_ref[...], bd_ref[...])
    o_ref[0, :, :NSA_KV_WIDTH] = k
    o_ref[0, :, NSA_KV_WIDTH:] = v


def _compress_prompt(x, pe, w, kg, bd1):
    b, nc, _ = x.shape
    return pl.pallas_call(
        _compress_prompt_kernel,
        grid=(b,),
        in_specs=[pl.BlockSpec((1, nc, CMP_ROW), lambda i: (i, 0, 0)), _full(pe.shape), _full(w.shape),
                  _full(kg.shape), _full(bd1.shape)],
        out_specs=pl.BlockSpec((1, nc, 2 * NSA_KV_WIDTH), lambda i: (i, 0, 0)),
        out_shape=jax.ShapeDtypeStruct((b, nc, 2 * NSA_KV_WIDTH), F32),
        compiler_params=_params("parallel"),
        name="compress_prompt",
    )(x, pe, w, kg, bd1)


def _compress_sample_kernel(pt_ref, pe_ref, w_ref, kg_ref, bd_ref, *rest, n_pages, per_page):
    pages, (o_ref, x_ref) = rest[:n_pages], rest[n_pages:]
    for p in range(n_pages):
        x_ref[p * per_page:(p + 1) * per_page, :] = pages[p][...]
    k, v = _compress_math(x_ref, pe_ref, w_ref, kg_ref[...], bd_ref[...])
    o_ref[0, :, :NSA_KV_WIDTH] = k
    o_ref[0, :, NSA_KV_WIDTH:] = v


def _compress_sample(page_table, cache, layer, pe, w, kg, bd1):
    ns, n_pages = page_table.shape
    per_page = cache.shape[2]
    nc = n_pages * per_page
    const = lambda shape: pl.BlockSpec(shape, lambda b, pt, n=len(shape): (0,) * n)
    page_specs = [pl.BlockSpec((None, None, per_page, CMP_ROW), lambda b, pt, p=p: (layer, pt[b, p], 0, 0))
                  for p in range(n_pages)]
    grid_spec = pltpu.PrefetchScalarGridSpec(
        num_scalar_prefetch=1,
        grid=(ns,),
        in_specs=[const(pe.shape), const(w.shape), const(kg.shape), const(bd1.shape)] + page_specs,
        out_specs=pl.BlockSpec((1, nc, 2 * NSA_KV_WIDTH), lambda b, pt: (b, 0, 0)),
        scratch_shapes=[pltpu.VMEM((nc, CMP_ROW), F32)],
    )
    return pl.pallas_call(
        functools.partial(_compress_sample_kernel, n_pages=n_pages, per_page=per_page),
        grid_spec=grid_spec,
        out_shape=jax.ShapeDtypeStruct((ns, nc, 2 * NSA_KV_WIDTH), F32),
        compiler_params=_params("arbitrary"),
        name="compress_sample",
    )(page_table, pe, w, kg, bd1, *([cache] * n_pages))


def _compressed_attend(qn, kc, tvec):
    nc = kc.shape[0]
    pad = jnp.zeros((LANES - nc, NSA_KV_WIDTH), F32)
    kck = jnp.concatenate([kc[:, :NSA_KV_WIDTH], pad], axis=0).astype(BF16)
    kcv = jnp.concatenate([kc[:, NSA_KV_WIDTH:], pad], axis=0).astype(BF16)
    lane = lax.broadcasted_iota(jnp.int32, (1, LANES), 1)
    valid = (((lane + 1) * CMP_BLOCK - 1) <= tvec) & (lane < nc)
    s = jnp.where(valid, _dot_nt(qn, kck), NEG)
    e = jnp.exp(s - jnp.max(s, axis=-1, keepdims=True))
    any_valid = jnp.max(valid.astype(F32), axis=-1, keepdims=True)
    p = e / jnp.sum(e, axis=-1, keepdims=True) * any_valid
    return p, _dot(p.astype(BF16), kcv)


def _select_blocks(pair, tvec, n_sel):
    lane = lax.broadcasted_iota(jnp.int32, (1, LANES), 1)
    even = (lane & 1) == 0
    cur2 = _div(tvec, SEL_BLOCK) * 2
    forced = even & ((lane == 0) | (lane == cur2) | (lane == cur2 - 2))
    valid = even & (lane * (SEL_BLOCK // 2) <= tvec)
    score = jnp.where(forced, 1e30, jnp.where(valid, pair, jnp.where(even, -1.0, -2.0)))
    lanef = lane.astype(F32)
    sel = jnp.zeros(pair.shape, F32)
    for _ in range(n_sel):
        best = jnp.max(score, axis=-1, keepdims=True)
        idx = jnp.min(jnp.where(score == best, lanef, 1e9), axis=-1, keepdims=True)
        hit = lanef == idx
        sel = jnp.where(hit, 1.0, sel)
        score = jnp.where(hit, -3.0, score)
    return sel


def _pair_importance(imp):
    return imp + pltpu.roll(imp, LANES - 1, 1)


def _softmax_tiles(q, tiles):
    scores = [jnp.where(msk, _dot_nt(q, k), NEG) for k, _, msk in tiles]
    m = functools.reduce(jnp.maximum, [jnp.max(s, axis=-1, keepdims=True) for s in scores])
    es = [jnp.exp(s - m) for s in scores]
    l = sum(jnp.sum(e, axis=-1, keepdims=True) for e in es)
    o = sum(_dot(e.astype(BF16), v) for e, (_, v, _) in zip(es, tiles))
    return o / l


def _flash(q, kv_ref, lo, hi, tk, mask_fn):
    r = q.shape[0]

    def body(j, carry):
        m, l, acc = carry
        off = pl.multiple_of(j * tk, tk)
        kv = kv_ref[pl.ds(off, tk), :]
        kpos = off + lax.broadcasted_iota(jnp.int32, (1, tk), 1)
        s = jnp.where(mask_fn(j, kpos), _dot_nt(q, kv[:, :NSA_KV_WIDTH]), NEG)
        m2 = jnp.maximum(m, jnp.max(s, axis=-1, keepdims=True))
        a = jnp.exp(m - m2)
        p = jnp.exp(s - m2)
        l = a * l + jnp.sum(p, axis=-1, keepdims=True)
        acc = a * acc + _dot(p.astype(BF16), kv[:, NSA_KV_WIDTH:])
        return m2, l, acc

    init = (jnp.full((r, 1), NEG, F32), jnp.zeros((r, 1), F32), jnp.zeros((r, NSA_KV_WIDTH), F32))
    _, l, acc = lax.fori_loop(lo, hi, body, init)
    return acc / l


def _nsa_prompt_kernel(qn_ref, qr_ref, gate_ref, kc_ref, sel_ref, win_ref, e_ref, o_ref,
                       selb_ref, winb_ref, *, tq, tk, n_sel):
    i = pl.program_id(1)

    @pl.when(i == 0)
    def _():
        selb_ref[...] = sel_ref[0].astype(BF16)
        winb_ref[...] = win_ref[0].astype(BF16)

    t0 = i * tq
    lane = lax.broadcasted_iota(jnp.int32, (1, LANES), 1)
    group_mask = [(lane < HEAD_DIM).astype(F32), (lane >= HEAD_DIM).astype(F32)]
    trow = lax.broadcasted_iota(jnp.int32, (tq, 1), 0)
    tvec = t0 + jnp.concatenate([trow] * 4, axis=0)
    tvec2 = t0 + jnp.concatenate([trow] * 2, axis=0)

    def stack(q):
        parts = [q[:, j * LANES:(j + 1) * LANES] * group_mask[g] for g in (0, 1) for j in (0, 1)]
        return (jnp.concatenate(parts, axis=0) * SCALE).astype(BF16)

    qn, qr = stack(qn_ref[0]), stack(qr_ref[0])
    p, o_cmp = _compressed_attend(qn, kc_ref[0], tvec)
    imp = jnp.concatenate([p[0:tq] + p[tq:2 * tq], p[2 * tq:3 * tq] + p[3 * tq:4 * tq]], axis=0)
    sel = _select_blocks(_pair_importance(imp), tvec2, n_sel)
    sel4 = jnp.concatenate([sel[:tq], sel[:tq], sel[tq:], sel[tq:]], axis=0).astype(BF16)

    hi = (t0 + tq - 1) // tk + 1
    o_sel = _flash(qr, selb_ref, 0, hi, tk,
                   lambda j, kpos: (_dot(sel4, e_ref[j]) > 0.5) & (kpos <= tvec))
    lo = jnp.maximum(t0 - WINDOW + 1, 0) // tk
    o_win = _flash(qr, winb_ref, lo, hi, tk,
                   lambda j, kpos: (kpos <= tvec) & (kpos > tvec - WINDOW))

    gate = gate_ref[0]
    halves = []
    for j in (0, 1):
        y = 0.0
        for g in (0, 1):
            rows = slice((2 * g + j) * tq, (2 * g + j + 1) * tq)
            yg = sum(gate[:, br * BRANCH_WIDTH + j * LANES: br * BRANCH_WIDTH + (j + 1) * LANES] * o[rows]
                     for br, o in enumerate((o_cmp, o_sel, o_win)))
            y = y + yg * group_mask[g]
        halves.append(y)
    o_ref[0] = jnp.concatenate(halves, axis=-1)


def _expand_matrix(n_tiles, tk):
    c = np.arange(LANES)[None, :, None]
    key = (np.arange(n_tiles)[:, None, None] * tk + np.arange(tk)[None, None, :])
    return jnp.asarray((c % 2 == 0) & (c // 2 == key // SEL_BLOCK), BF16)


def _nsa_prompt(qn, qr, gate, kc, sel, win):
    b, t, _ = qn.shape
    tq = tk = min(ATT_TQ, t)
    n_sel = min(N_SELECT, t // SEL_BLOCK)
    e = _expand_matrix(t // tk, tk)
    qspec = lambda w: pl.BlockSpec((1, tq, w), lambda bi, i: (bi, i, 0))
    seq = lambda n, w: pl.BlockSpec((1, n, w), lambda bi, i: (bi, 0, 0))
    return pl.pallas_call(
        functools.partial(_nsa_prompt_kernel, tq=tq, tk=tk, n_sel=n_sel),
        grid=(b, t // tq),
        in_specs=[qspec(256), qspec(256), qspec(768), seq(kc.shape[1], 256), seq(t, 256), seq(t, 256),
                  _full(e.shape)],
        out_specs=qspec(256),
        out_shape=jax.ShapeDtypeStruct((b, t, BRANCH_WIDTH), F32),
        scratch_shapes=[pltpu.VMEM((t, 256), BF16), pltpu.VMEM((t, 256), BF16)],
        compiler_params=_params("parallel", "arbitrary"),
        name="nsa_prompt",
    )(qn, qr, gate, kc, sel, win, e)


def _nsa_sample_kernel(pt_ref, qn_ref, qr_ref, gate_ref, kc_ref, seln_ref, win_ref, winn_ref, e_ref, *rest,
                       n_pages, nd, past, n_sel):
    pages, (o_ref, wino_ref, newt_ref) = rest[:n_pages], rest[n_pages:]

    @pl.when(pl.program_id(0) == 0)
    def _():
        newt_ref[...] = jnp.zeros_like(newt_ref)

    newt_ref[0, 0:nd, :] = seln_ref[0]
    newt_ref[1, 0:nd, :] = winn_ref[0]
    rows = 4 * nd
    row = lax.broadcasted_iota(jnp.int32, (rows, 1), 0)
    lane = lax.broadcasted_iota(jnp.int32, (1, LANES), 1)
    group_mask = ((lane >> 6) == _div(row, 2 * nd)).astype(F32)
    tau = _mod(row, nd)
    tvec = past + tau
    qn = (qn_ref[0] * SCALE * group_mask).astype(BF16)
    qr = (qr_ref[0] * SCALE * group_mask).astype(BF16)

    p, o_cmp = _compressed_attend(qn, kc_ref[0], tvec)
    imp = p + pltpu.roll(p, nd, 0)
    sel = _select_blocks(_pair_importance(imp), tvec, n_sel)
    sel4 = jnp.where(_mod(_div(row, nd), 2) == 1, sel, pltpu.roll(sel, rows - nd, 0)).astype(BF16)

    page = pages[0].shape[0]
    tiles = []
    for pi in range(n_pages):
        pg = pages[pi][...].astype(BF16)
        tiles.append((pg[:, :NSA_KV_WIDTH], pg[:, NSA_KV_WIDTH:], _dot(sel4, e_ref[pi]) > 0.5))
    newt = newt_ref[0].astype(BF16)
    tiles.append((newt[:, :NSA_KV_WIDTH], newt[:, NSA_KV_WIDTH:],
                  (_dot(sel4, e_ref[n_pages]) > 0.5) & (lane[:, :page] <= tau)))
    o_sel = _softmax_tiles(qr, tiles)

    wb = win_ref.shape[1]
    tiles = []
    for wi in range(wb // page):
        wt = win_ref[0, wi * page:(wi + 1) * page, :].astype(BF16)
        r = wi * page + lane[:, :page]
        tiles.append((wt[:, :NSA_KV_WIDTH], wt[:, NSA_KV_WIDTH:], (r > tau + (wb - WINDOW)) & (r >= wb - past)))
    newt = newt_ref[1].astype(BF16)
    tiles.append((newt[:, :NSA_KV_WIDTH], newt[:, NSA_KV_WIDTH:], lane[:, :page] <= tau))
    o_win = _softmax_tiles(qr, tiles)

    y = (gate_ref[0, 0] * o_cmp + gate_ref[0, 1] * o_sel + gate_ref[0, 2] * o_win) * group_mask
    o_ref[0] = y[0:2 * nd] + y[2 * nd:4 * nd]

    wino_ref[0, 0:wb - nd, :] = win_ref[0, nd:wb, :]
    wino_ref[0, wb - nd:wb, :] = winn_ref[0]


def _nsa_sample(page_table, qn16, qr16, gate16, kc, sel_new, win_state, win_new, cache_sel, layer, past):
    ns, n_pages = page_table.shape
    nd = sel_new.shape[1]
    assert nd == 4
    page = cache_sel.shape[2]
    wb = win_state.shape[1]
    assert wb % page == 0 and page == LANES
    n_sel = min(N_SELECT, -(-(past + nd) // SEL_BLOCK))
    e = _expand_matrix(n_pages + 1, page)
    per_seq = lambda *s: pl.BlockSpec((1,) + s, lambda b, pt, n=len(s): (b,) + (0,) * n)
    page_specs = [pl.BlockSpec((None, None, page, 256), lambda b, pt, p=p: (layer, pt[b, p], 0, 0))
                  for p in range(n_pages)]
    grid_spec = pltpu.PrefetchScalarGridSpec(
        num_scalar_prefetch=1,
        grid=(ns,),
        in_specs=[per_seq(16, LANES), per_seq(16, LANES), per_seq(3, 16, LANES), per_seq(kc.shape[1], 256),
                  per_seq(nd, 256), per_seq(wb, 256), per_seq(nd, 256),
                  pl.BlockSpec(e.shape, lambda b, pt: (0, 0, 0))] + page_specs,
        out_specs=[per_seq(2 * nd, LANES), per_seq(wb, 256)],
        scratch_shapes=[pltpu.VMEM((2, page, 256), F32)],
    )
    return pl.pallas_call(
        functools.partial(_nsa_sample_kernel, n_pages=n_pages, nd=nd, past=past, n_sel=n_sel),
        grid_spec=grid_spec,
        out_shape=[jax.ShapeDtypeStruct((ns, 2 * nd, LANES), F32), jax.ShapeDtypeStruct((ns, wb, 256), F32)],
        compiler_params=_params("arbitrary"),
        name="nsa_sample",
    )(page_table, qn16, qr16, gate16, kc, sel_new, win_state, win_new, e, *([cache_sel] * n_pages))


_SC_HALO, _CC_HALO = 8, 32


def _conv_kernel(u_ref, scb_ref, a_ref, uprev_ref, aprev_ref, stsc_ref, stcc_ref,
                 scw_ref, ccw_ref, ccb_ref, ccg_ref, ccbeta_ref,
                 yc_ref, yd_ref, nsc_ref, ncc_ref, fu_ref, fa_ref, *, tc):
    c = pl.program_id(1)

    @pl.when(c == 0)
    def _():
        fu_ref[:, _SC_HALO - (SC_CONV_LEN - 1):_SC_HALO, :] = stsc_ref[...]
        fa_ref[:, _CC_HALO - (CC_CONV_LEN - 1):_CC_HALO, :] = stcc_ref[...]

    @pl.when(c > 0)
    def _():
        fu_ref[:, 0:_SC_HALO, :] = uprev_ref[...]
        fa_ref[:, 0:_CC_HALO, :] = aprev_ref[...]

    fu_ref[:, _SC_HALO:, :] = u_ref[...]
    fa_ref[:, _CC_HALO:, :] = a_ref[...]
    sub = min(CONV_SUB, tc)
    for s in range(tc // sub):
        r0 = s * sub
        acc = 0.0
        for i in range(SC_CONV_LEN):
            o = r0 + i + _SC_HALO - (SC_CONV_LEN - 1)
            acc = acc + scw_ref[i:i + 1, :] * fu_ref[:, o:o + sub, :]
        yc_ref[:, r0:r0 + sub, :] = scb_ref[:, r0:r0 + sub, :] * acc
        acc = 0.0
        for i in range(CC_CONV_LEN):
            o = r0 + i + _CC_HALO - (CC_CONV_LEN - 1)
            acc = acc + ccw_ref[i:i + 1, :] * fa_ref[:, o:o + sub, :]
        z = acc + ccb_ref[...]
        mu = jnp.mean(z, axis=-1, keepdims=True)
        var = jnp.mean(jnp.square(z - mu), axis=-1, keepdims=True)
        ln = (z - mu) * lax.rsqrt(var + NORM_EPS) * ccg_ref[...] + ccbeta_ref[...]
        yd_ref[:, r0:r0 + sub, :] = ln * jax.nn.sigmoid(ln)
    nsc_ref[...] = fu_ref[:, tc + _SC_HALO - (SC_CONV_LEN - 1):tc + _SC_HALO, :]
    ncc_ref[...] = fa_ref[:, tc + _CC_HALO - (CC_CONV_LEN - 1):tc + _CC_HALO, :]


def _conv(u, scb, a, st_sc, st_cc, scw, ccw, ccb, ccg, ccbeta, nb):
    b, t, w = u.shape
    tc = min(CONV_TC, t)
    chunked = t > tc
    if chunked:
        uprev, aprev = u, a
        prev_u = pl.BlockSpec((nb, _SC_HALO, w), lambda bi, c: (bi, jnp.maximum(c * (tc // _SC_HALO) - 1, 0), 0))
        prev_a = pl.BlockSpec((nb, _CC_HALO, w), lambda bi, c: (bi, jnp.maximum(c * (tc // _CC_HALO) - 1, 0), 0))
    else:
        uprev, aprev = jnp.zeros((b, _SC_HALO, w), F32), jnp.zeros((b, _CC_HALO, w), F32)
        prev_u = pl.BlockSpec((nb, _SC_HALO, w), lambda bi, c: (bi, 0, 0))
        prev_a = pl.BlockSpec((nb, _CC_HALO, w), lambda bi, c: (bi, 0, 0))
    cur = pl.BlockSpec((nb, tc, w), lambda bi, c: (bi, c, 0))
    state = lambda n: pl.BlockSpec((nb, n, w), lambda bi, c: (bi, 0, 0))
    wspec = lambda arr: pl.BlockSpec(arr.shape, lambda bi, c: (0, 0))
    return pl.pallas_call(
        functools.partial(_conv_kernel, tc=tc),
        grid=(b // nb, t // tc),
        in_specs=[cur, cur, cur, prev_u, prev_a, state(SC_CONV_LEN - 1), state(CC_CONV_LEN - 1),
                  wspec(scw), wspec(ccw), wspec(ccb), wspec(ccg), wspec(ccbeta)],
        out_specs=[cur, cur, state(SC_CONV_LEN - 1), state(CC_CONV_LEN - 1)],
        out_shape=[jax.ShapeDtypeStruct((b, t, w), F32), jax.ShapeDtypeStruct((b, t, w), F32),
                   jax.ShapeDtypeStruct((b, SC_CONV_LEN - 1, w), F32),
                   jax.ShapeDtypeStruct((b, CC_CONV_LEN - 1, w), F32)],
        scratch_shapes=[pltpu.VMEM((nb, _SC_HALO + tc, w), F32), pltpu.VMEM((nb, _CC_HALO + tc, w), F32)],
        compiler_params=_params("parallel", "arbitrary"),
        name="conv",
    )(u, scb, a, uprev, aprev, st_sc, st_cc, scw, ccw, ccb, ccg, ccbeta)


def _merge_kernel(x_ref, ya_ref, yb_ref, yc_ref, yd_ref, g_ref, wg_ref, bg_ref, wb_ref, wo_ref, o_ref):
    x = x_ref[...]
    d = x.shape[-1]
    h = _rms(x, g_ref[...]).astype(BF16)
    merged = 0.0
    for n, y_ref in enumerate((ya_ref, yb_ref, yc_ref, yd_ref)):
        gate = jax.nn.sigmoid(_dot(h, wg_ref[:, n * d:(n + 1) * d]) + bg_ref[:, n * d:(n + 1) * d])
        merged = merged + gate * _dot(y_ref[...].astype(BF16), wb_ref[n])
    o_ref[...] = x + _dot(merged.astype(BF16), wo_ref[...])


def _merge(x, ya, yb, yc, yd, g, wg, bg, wb, wo):
    n, d = x.shape
    tm = min(TOKEN_TILE, n)
    tok = lambda w: pl.BlockSpec((tm, w), lambda i: (i, 0))
    return pl.pallas_call(
        _merge_kernel,
        grid=(n // tm,),
        in_specs=[tok(d)] + [tok(BRANCH_WIDTH)] * 4 + [_full(a.shape) for a in (g, wg, bg, wb, wo)],
        out_specs=tok(d),
        out_shape=jax.ShapeDtypeStruct((n, d), F32),
        compiler_params=_params("parallel"),
        name="merge",
    )(x, ya, yb, yc, yd, g, wg, bg, wb, wo)


def _w_in_columns():
    splits = (BRANCH_WIDTH,) * 4 + (NSA_KV_WIDTH,) * 6 + (12,) + (BRANCH_WIDTH,) * 5
    offs = np.concatenate([[0], np.cumsum(splits)])
    (o_sbq, o_sbk, _, o_nq, o_ck, _, o_sk, _, o_wk, _, o_ng, o_scb, o_scc, o_sch, o_cca, o_ccg) = offs[:-1]
    r = lambda o, n: np.arange(o, o + n)
    jg_heads = np.concatenate([r((g * 2 + j) * HEAD_DIM, HEAD_DIM) for j in (0, 1) for g in (0, 1)])
    gates = np.concatenate([np.full(HEAD_DIM, o_ng + g * 6 + j * 3 + br)
                            for br in range(3) for j in (0, 1) for g in (0, 1)])
    cols = np.concatenate([r(o_sbq, 256), r(o_sbk, 512), o_nq + jg_heads, r(o_ck, 256), r(o_sk, 256),
                           r(o_wk, 256), gates, r(o_scb, 256), r(o_scc, 256), r(o_sch, 256), r(o_cca, 256),
                           r(o_ccg, 256)])
    assert cols.shape[0] == _P_END
    return cols, jg_heads


def _rope_tables(pos):
    half = HEAD_DIM // 2
    inv_freq = jnp.exp(-math.log(ROPE_THETA) * jnp.arange(half, dtype=F32) / half)
    ang = pos.astype(F32)[:, None] * inv_freq[None, :]
    cos, sin = jnp.cos(ang), jnp.sin(ang)
    reps = BRANCH_WIDTH // HEAD_DIM
    return jnp.tile(jnp.concatenate([cos, cos], -1), (1, reps)), jnp.tile(jnp.concatenate([-sin, sin], -1), (1, reps))


def _rows16(a, nd):
    ns = a.shape[0]
    a = a.reshape(ns, nd, 2, LANES).transpose(0, 2, 1, 3)
    return jnp.broadcast_to(a[:, None], (ns, 2, 2, nd, LANES)).reshape(ns, 4 * nd, LANES)


def kernel(x_prompt, x_sample, cache_sb_kv, cache_cmp_kv, cache_sel_kv, state_win_kv, state_conv_sc,
           state_conv_cc, page_table, norm_ffn1, w_ffn1_gu, w_ffn1_down, norm_mix, w_in, nsa_q_gain,
           nsa_k_gain, nsa_cmp_pe, nsa_cmp_w, sc_conv_w, cc_conv_w, cc_conv_b, cc_norm_g, cc_norm_b,
           w_branch, w_gate, b_gate, w_out, norm_ffn2, w_ffn2_gu, w_ffn2_down):
    b, t, d = x_prompt.shape
    ns, nd, _ = x_sample.shape
    depth, n_phys, page = cache_sb_kv.shape[:3]
    n_pages = page_table.shape[1]
    past = n_pages * page
    ffn_dim = w_ffn1_down.shape[1]
    n_chunk = ffn_dim // FFN_CHUNK
    wb = state_win_kv.shape[2]

    def gu(w):
        return w.reshape(depth, d, 2, n_chunk, FFN_CHUNK).transpose(0, 2, 3, 1, 4).astype(BF16)

    wgu1, wgu2 = gu(w_ffn1_gu), gu(w_ffn2_gu)
    wd1 = w_ffn1_down.reshape(depth, n_chunk, FFN_CHUNK, d).astype(BF16)
    wd2 = w_ffn2_down.reshape(depth, n_chunk, FFN_CHUNK, d).astype(BF16)
    cols, jg_heads = _w_in_columns()
    w_in_r = w_in[:, :, cols].astype(BF16)
    w_gate_b = w_gate.astype(BF16)
    w_branch_b = w_branch.at[:, 1].set(w_branch[:, 1][:, jg_heads, :]).astype(BF16)
    w_out_b = w_out.astype(BF16)
    eye2 = jnp.eye(2, dtype=F32)
    w_cmp = jnp.einsum("Lklde,kK,gG->LlkgdKGe", nsa_cmp_w, eye2, eye2).reshape(depth, CMP_ROW, 256).astype(BF16)
    pe_row = jnp.broadcast_to(nsa_cmp_pe.transpose(0, 2, 1, 3)[:, :, :, None, :],
                              (depth, CMP_BLOCK, 2, 2, HEAD_DIM)).reshape(depth, 1, CMP_ROW)
    row2 = lambda a: a.reshape(depth, 1, -1)
    n1, nm, n2, bg = row2(norm_ffn1), row2(norm_mix), row2(norm_ffn2), row2(b_gate)
    qg = jnp.tile(row2(nsa_q_gain), (1, 1, 4))
    kg = jnp.tile(row2(nsa_k_gain), (1, 1, 2))
    ccb, ccg, ccbeta = row2(cc_conv_b), row2(cc_norm_g), row2(cc_norm_b)
    hd = np.arange(BRANCH_WIDTH) // HEAD_DIM
    bd = jnp.asarray(hd[:, None] == hd[None, :], BF16)
    bd1 = bd[:LANES, :LANES]
    cos_p, sin_p = _rope_tables(jnp.arange(t, dtype=jnp.int32))
    cos_s, sin_s = _rope_tables(past + jnp.arange(ns * nd, dtype=jnp.int32) % nd)

    cache_sb = cache_sb_kv.reshape(depth, n_phys, page, 2 * BRANCH_WIDTH)
    cache_cmp = cache_cmp_kv.reshape(depth, n_phys, page // CMP_BLOCK, CMP_ROW)
    cache_sel = cache_sel_kv.reshape(depth, n_phys, page, 256)
    win_state = state_win_kv.reshape(depth, ns, wb, 256)

    xp = x_prompt.reshape(b * t, d)
    xs = x_sample.reshape(ns * nd, d)
    zeros_sc = jnp.zeros((b, SC_CONV_LEN - 1, BRANCH_WIDTH), F32)
    zeros_cc = jnp.zeros((b, CC_CONV_LEN - 1, BRANCH_WIDTH), F32)
    outs_p = [[] for _ in range(6)]
    outs_s = [[] for _ in range(6)]
    for l in range(depth):
        xp = _ffn(xp, n1[l], wgu1[l], wd1[l])
        xs = _ffn(xs, n1[l], wgu1[l], wd1[l])

        (sbq, sbkv, qn, qr, cmp_, sel, win, gate, scb, scu, cca) = _proj(
            xp, nm[l], w_in_r[l], cos_p, sin_p, qg[l], kg[l], bd)
        seq = lambda a: a.reshape(b, t, a.shape[-1])
        ya = _sb_prompt(seq(sbq), seq(sbkv))
        kc = _compress_prompt(cmp_.reshape(b, t // CMP_BLOCK, CMP_ROW), pe_row[l], w_cmp[l], kg[l], bd1)
        yb = _nsa_prompt(seq(qn), seq(qr), seq(gate), kc, seq(sel), seq(win))
        yc, yd, nsc, ncc = _conv(seq(scu), seq(scb), seq(cca), zeros_sc, zeros_cc, sc_conv_w[l], cc_conv_w[l],
                                 ccb[l], ccg[l], ccbeta[l], nb=1)
        flat = lambda a: a.reshape(b * t, a.shape[-1])
        xp = _merge(xp, flat(ya), flat(yb), flat(yc), flat(yd), nm[l], w_gate_b[l], bg[l], w_branch_b[l],
                    w_out_b[l])
        for lst, a in zip(outs_p, (seq(sbkv), seq(cmp_), seq(sel), seq(win)[:, t - min(WINDOW, t):], nsc, ncc)):
            lst.append(a)

        (sbq, sbkv, qn, qr, cmp_, sel, win, gate, scb, scu, cca) = _proj(
            xs, nm[l], w_in_r[l], cos_s, sin_s, qg[l], kg[l], bd)
        seq = lambda a: a.reshape(ns, nd, a.shape[-1])
        q16 = jnp.tile(seq(sbq), (1, 4, 1))
        ya = _sb_sample(page_table, q16, seq(sbkv), cache_sb, l)
        kc = _compress_sample(page_table, cache_cmp, l, pe_row[l], w_cmp[l], kg[l], bd1)
        gate16 = jnp.stack([_rows16(seq(gate)[:, :, br * 256:(br + 1) * 256], nd) for br in range(3)], axis=1)
        y8, win_out = _nsa_sample(page_table, _rows16(seq(qn), nd), _rows16(seq(qr), nd), gate16, kc,
                                  seq(sel), win_state[l], seq(win), cache_sel, l, past)
        yb = y8.reshape(ns, 2, nd, LANES).transpose(0, 2, 1, 3).reshape(ns, nd, BRANCH_WIDTH)
        yc, yd, nsc, ncc = _conv(seq(scu), seq(scb), seq(cca), state_conv_sc[l], state_conv_cc[l],
                                 sc_conv_w[l], cc_conv_w[l], ccb[l], ccg[l], ccbeta[l], nb=min(8, ns))
        flat = lambda a: a.reshape(ns * nd, a.shape[-1])
        xs = _merge(xs, flat(ya), flat(yb), flat(yc), flat(yd), nm[l], w_gate_b[l], bg[l], w_branch_b[l],
                    w_out_b[l])
        for lst, a in zip(outs_s, (seq(sbkv), seq(cmp_), seq(sel), win_out, nsc, ncc)):
            lst.append(a)

        xp = _ffn(xp, n2[l], wgu2[l], wd2[l])
        xs = _ffn(xs, n2[l], wgu2[l], wd2[l])

    def kv5(a, heads):
        a = jnp.stack(a)
        return a.reshape(a.shape[:3] + (2, heads, HEAD_DIM))

    sb_heads, nsa_kv_heads = BRANCH_WIDTH // HEAD_DIM, NSA_KV_WIDTH // HEAD_DIM
    return (xp.reshape(b, t, d), xs.reshape(ns, nd, d),
            kv5(outs_p[0], sb_heads), kv5(outs_s[0], sb_heads),
            kv5(outs_p[1], nsa_kv_heads), kv5(outs_s[1], nsa_kv_heads),
            kv5(outs_p[2], nsa_kv_heads), kv5(outs_s[2], nsa_kv_heads),
            kv5(outs_p[3], nsa_kv_heads), kv5(outs_s[3], nsa_kv_heads),
            jnp.stack(outs_p[4]), jnp.stack(outs_s[4]), jnp.stack(outs_p[5]), jnp.stack(outs_s[5]))
```

```python
import functools
import math

import numpy as np
import jax
import jax.numpy as jnp
from jax import lax
from jax.experimental import pallas as pl
from jax.experimental.pallas import tpu as pltpu

F32, BF16 = jnp.float32, jnp.bfloat16

HEAD_DIM = 64
BRANCH_WIDTH = 256
N_BRANCH = 4
NSA_KV_WIDTH = 128
CMP_BLOCK = 32
SEL_BLOCK = 64
N_SELECT = 16
WINDOW = 512
SC_CONV_LEN = 3
CC_CONV_LEN = 31
ROPE_THETA = 10000.0
NORM_EPS = 1e-6
NEG = -1e30
SCALE = HEAD_DIM ** -0.5

LANES = 128
SUBLANES = 8
VMEM_LIMIT_BYTES = 56 * 1024 * 1024
TOKEN_TILE = 512
FFN_CHUNK = 256
ATT_TQ = 128
SB_TK = 256
NSA_TK = 128
CONV_TC = 256
CONV_SUB = 64

_NT = (((1,), (1,)), ((), ()))


def _params(*sem):
    return pltpu.CompilerParams(dimension_semantics=sem, vmem_limit_bytes=VMEM_LIMIT_BYTES)


def _full(shape):
    n = len(shape)
    return pl.BlockSpec(shape, lambda *_, n=n: (0,) * n)


def _dot(a, b):
    return jnp.dot(a, b, preferred_element_type=F32)


def _dot_nt(a, b):
    return lax.dot_general(a, b, _NT, preferred_element_type=F32)


def _div(x, n):
    assert n & (n - 1) == 0
    return x >> (n.bit_length() - 1)


def _mod(x, n):
    assert n & (n - 1) == 0
    return x & (n - 1)


def _rms(x, g):
    return x * lax.rsqrt(jnp.mean(x * x, axis=-1, keepdims=True) + NORM_EPS) * g


def _split(x):
    hi = x.astype(BF16)
    return hi, (x - hi.astype(F32)).astype(BF16)


def _split_dot(x, m):
    hi, lo = _split(x)
    return _dot(hi, m) + _dot(lo, m)


def _head_rms(x, bd, g):
    ms = _split_dot(x * x, bd) * (1.0 / HEAD_DIM)
    return x * lax.rsqrt(ms + NORM_EPS) * g


def _rope(x, cos, sin_signed):
    w = x.shape[-1]
    lane = lax.broadcasted_iota(jnp.int32, x.shape, 1)
    first_half = (lane & (HEAD_DIM - 1)) < HEAD_DIM // 2
    rot = jnp.where(first_half, pltpu.roll(x, w - HEAD_DIM // 2, 1), pltpu.roll(x, HEAD_DIM // 2, 1))
    return x * cos + rot * sin_signed


def _ffn_kernel(x_ref, g_ref, wgu_ref, wd_ref, o_ref, h_ref, acc_ref):
    h_ref[...] = _rms(x_ref[...], g_ref[...]).astype(BF16)
    acc_ref[...] = jnp.zeros_like(acc_ref)

    def body(c, carry):
        h = h_ref[...]
        g = _dot(h, wgu_ref[0, c])
        u = _dot(h, wgu_ref[1, c])
        a = (g * jax.nn.sigmoid(g) * u).astype(BF16)
        acc_ref[...] += _dot(a, wd_ref[c])
        return carry

    lax.fori_loop(0, wd_ref.shape[0], body, 0)
    o_ref[...] = x_ref[...] + 0.5 * acc_ref[...]


def _ffn(x, g, wgu, wd):
    n, d = x.shape
    tm = min(TOKEN_TILE, n)
    return pl.pallas_call(
        _ffn_kernel,
        grid=(n // tm,),
        in_specs=[pl.BlockSpec((tm, d), lambda i: (i, 0)), _full(g.shape), _full(wgu.shape), _full(wd.shape)],
        out_specs=pl.BlockSpec((tm, d), lambda i: (i, 0)),
        out_shape=jax.ShapeDtypeStruct((n, d), F32),
        scratch_shapes=[pltpu.VMEM((tm, d), BF16), pltpu.VMEM((tm, d), F32)],
        compiler_params=_params("parallel"),
        name="ffn",
    )(x, g, wgu, wd)


_P_SBQ, _P_SBKV, _P_NQ, _P_CMP, _P_SEL, _P_WIN, _P_GATE = 0, 256, 768, 1024, 1280, 1536, 1792
_P_SCB, _P_SCC, _P_SCH, _P_CCA, _P_CCG, _P_END = 2560, 2816, 3072, 3328, 3584, 3840


def _proj_kernel(x_ref, g_ref, w_ref, cos_ref, sin_ref, qg_ref, kg_ref, bd_ref,
                 sbq_ref, sbkv_ref, qn_ref, qr_ref, cmp_ref, sel_ref, win_ref, gate_ref,
                 scb_ref, scu_ref, cca_ref):
    h = _rms(x_ref[...], g_ref[...]).astype(BF16)

    def mm(a, b):
        return _dot(h, w_ref[:, a:b])

    cos, sin, bd = cos_ref[...], sin_ref[...], bd_ref[...]
    sbq_ref[...] = mm(_P_SBQ, _P_SBKV)
    sbkv_ref[...] = mm(_P_SBKV, _P_NQ)
    qn = _head_rms(mm(_P_NQ, _P_CMP), bd, qg_ref[...])
    qn_ref[...] = qn
    qr_ref[...] = _rope(qn, cos, sin)
    cmp_ref[...] = mm(_P_CMP, _P_SEL)
    bd1, cos1, sin1 = bd[:LANES, :LANES], cos[:, :LANES], sin[:, :LANES]
    for a, ref in ((_P_SEL, sel_ref), (_P_WIN, win_ref)):
        kv = mm(a, a + 2 * NSA_KV_WIDTH)
        ref[:, :NSA_KV_WIDTH] = _rope(_head_rms(kv[:, :NSA_KV_WIDTH], bd1, kg_ref[...]), cos1, sin1)
        ref[:, NSA_KV_WIDTH:] = kv[:, NSA_KV_WIDTH:]
    gate_ref[...] = jax.nn.sigmoid(mm(_P_GATE, _P_SCB))
    scb_ref[...] = mm(_P_SCB, _P_SCC)
    scu_ref[...] = mm(_P_SCC, _P_SCH) * mm(_P_SCH, _P_CCA)
    cca_ref[...] = mm(_P_CCA, _P_CCG) * jax.nn.sigmoid(mm(_P_CCG, _P_END))


def _proj(x, g, w, cos, sin, qg, kg, bd):
    n, d = x.shape
    tm = min(TOKEN_TILE, n)
    n_rope_tiles = cos.shape[0] // tm
    widths = (256, 512, 256, 256, 256, 256, 256, 768, 256, 256, 256)
    tok = lambda wd_: pl.BlockSpec((tm, wd_), lambda i: (i, 0))
    rope = pl.BlockSpec((tm, cos.shape[1]), lambda i: (i % n_rope_tiles, 0))
    return pl.pallas_call(
        _proj_kernel,
        grid=(n // tm,),
        in_specs=[tok(d), _full(g.shape), _full(w.shape), rope, rope, _full(qg.shape), _full(kg.shape),
                  _full(bd.shape)],
        out_specs=[tok(wd_) for wd_ in widths],
        out_shape=[jax.ShapeDtypeStruct((n, wd_), F32) for wd_ in widths],
        compiler_params=_params("parallel"),
        name="proj",
    )(x, g, w, cos, sin, qg, kg, bd)


def _sb_logs(z):
    sp = jnp.log1p(jnp.exp(-jnp.abs(z)))
    return jnp.minimum(z, 0.0) - sp, -jnp.maximum(z, 0.0) - sp


def _sb_tile_t(k, vt, q_t, valid, c, tri_t):
    log_beta, log_keep = _sb_logs(_dot(k, q_t))
    if valid is not None:
        log_keep = jnp.where(valid, log_keep, 0.0)
    hi, lo = _split(log_keep)
    w = jnp.exp(log_beta + _dot(tri_t, hi) + _dot(tri_t, lo) + c)
    if valid is not None:
        w = jnp.where(valid, w, 0.0)
    return _dot(vt, w.astype(BF16)), c + jnp.sum(log_keep, axis=0, keepdims=True)


def _sb_prompt_kernel(q_ref, kv_ref, tri_ref, o_ref, kb_ref, vtb_ref, acc_ref, *, tq, tk):
    i = pl.program_id(1)
    n_all = kb_ref.shape[0]

    @pl.when(i == 0)
    def _():
        for j in range(n_all):
            tile = kv_ref[0, j * tk:(j + 1) * tk, :]
            kb_ref[j] = tile[:, :BRANCH_WIDTH].astype(BF16)
            vtb_ref[j] = tile[:, BRANCH_WIDTH:].T.astype(BF16)

    t0 = i * tq
    feat = lax.broadcasted_iota(jnp.int32, (BRANCH_WIDTH, 1), 0)
    q_feat = q_ref[0].T * SCALE
    q_t = jnp.concatenate([jnp.where(_div(feat, HEAD_DIM) == h, q_feat, 0.0) for h in range(4)],
                          axis=1).astype(BF16)
    tlane = lax.broadcasted_iota(jnp.int32, (1, tq), 1)
    tvec = t0 + jnp.concatenate([tlane] * 4, axis=1)
    tri_t = tri_ref[...]
    j_diag = (t0 + tq - 1) // tk
    kpos = j_diag * tk + lax.broadcasted_iota(jnp.int32, (tk, 1), 0)
    acc, c = _sb_tile_t(kb_ref[j_diag], vtb_ref[j_diag], q_t, kpos < tvec,
                        jnp.zeros((1, 4 * tq), F32), tri_t)
    acc_ref[...] = acc

    def body(it, c):
        j = j_diag - 1 - it
        pv, c = _sb_tile_t(kb_ref[j], vtb_ref[j], q_t, None, c, tri_t)
        acc_ref[...] += pv
        return c

    lax.fori_loop(0, j_diag, body, c)
    y_t = jnp.concatenate([acc_ref[h * HEAD_DIM:(h + 1) * HEAD_DIM, h * tq:(h + 1) * tq] for h in range(4)],
                          axis=0)
    o_ref[0] = y_t.T


def _tri(n, upper):
    r = np.arange(n)
    m = r[:, None] < r[None, :] if upper else r[:, None] > r[None, :]
    return jnp.asarray(m, BF16)


def _sb_prompt(q, kv):
    b, t, _ = q.shape
    tq, tk = min(ATT_TQ, t), min(SB_TK, t)
    assert tk % tq == 0
    return pl.pallas_call(
        functools.partial(_sb_prompt_kernel, tq=tq, tk=tk),
        grid=(b, t // tq),
        in_specs=[pl.BlockSpec((1, tq, BRANCH_WIDTH), lambda bi, i: (bi, i, 0)),
                  pl.BlockSpec((1, t, 2 * BRANCH_WIDTH), lambda bi, i: (bi, 0, 0)),
                  _full((tk, tk))],
        out_specs=pl.BlockSpec((1, tq, BRANCH_WIDTH), lambda bi, i: (bi, i, 0)),
        out_shape=jax.ShapeDtypeStruct((b, t, BRANCH_WIDTH), F32),
        scratch_shapes=[pltpu.VMEM((t // tk, tk, BRANCH_WIDTH), BF16),
                        pltpu.VMEM((t // tk, BRANCH_WIDTH, tk), BF16),
                        pltpu.VMEM((BRANCH_WIDTH, 4 * tq), F32)],
        compiler_params=_params("parallel", "arbitrary"),
        name="sb_prompt",
    )(q, kv, _tri(tk, upper=True))


def _sb_tile(qm, kt, vt, valid, c, tri):
    log_beta, log_keep = _sb_logs(_dot(qm, kt))
    if valid is not None:
        log_keep = jnp.where(valid, log_keep, 0.0)
    w = jnp.exp(log_beta + _split_dot(log_keep, tri) + c)
    if valid is not None:
        w = jnp.where(valid, w, 0.0)
    return _dot_nt(w.astype(BF16), vt), c + jnp.sum(log_keep, axis=-1, keepdims=True)


def _sb_sample_kernel(pt_ref, q_ref, new_ref, tri_ref, *rest, n_pages, nd):
    pages, (o_ref, newt_ref) = rest[:n_pages], rest[n_pages:]

    @pl.when(pl.program_id(0) == 0)
    def _():
        newt_ref[...] = jnp.zeros_like(newt_ref)

    newt_ref[0:nd, :] = new_ref[0]
    rows = 4 * nd
    row = lax.broadcasted_iota(jnp.int32, (rows, 1), 0)
    lane = lax.broadcasted_iota(jnp.int32, (1, BRANCH_WIDTH), 1)
    head_mask = (_div(lane, HEAD_DIM) == _div(row, nd)).astype(F32)
    qm = (q_ref[0] * SCALE * head_mask).astype(BF16)
    tri = tri_ref[...]
    new_t = newt_ref[...].T.astype(BF16)
    key = lax.broadcasted_iota(jnp.int32, (1, new_t.shape[1]), 1)
    acc, c = _sb_tile(qm, new_t[:BRANCH_WIDTH], new_t[BRANCH_WIDTH:], key < _mod(row, nd),
                      jnp.zeros((rows, 1), F32), tri)
    for p in reversed(range(n_pages)):
        pg = pages[p][...].astype(BF16)
        pv, c = _sb_tile(qm, pg[:BRANCH_WIDTH], pg[BRANCH_WIDTH:], None, c, tri)
        acc = acc + pv
    m = acc * head_mask
    s = m[0:8] + m[8:16]
    o_ref[0] = s + pltpu.roll(s, 4, 0)


def _sb_sample(page_table, q16, new, cache, layer):
    ns, n_pages = page_table.shape
    nd = new.shape[1]
    assert nd == 4 and q16.shape[1] == 16
    width, page = cache.shape[2:]
    page_specs = [pl.BlockSpec((None, None, width, page), lambda b, pt, p=p: (layer, pt[b, p], 0, 0))
                  for p in range(n_pages)]
    grid_spec = pltpu.PrefetchScalarGridSpec(
        num_scalar_prefetch=1,
        grid=(ns,),
        in_specs=[pl.BlockSpec((1, 16, BRANCH_WIDTH), lambda b, pt: (b, 0, 0)),
                  pl.BlockSpec((1, nd, width), lambda b, pt: (b, 0, 0)),
                  pl.BlockSpec((page, page), lambda b, pt: (0, 0))] + page_specs,
        out_specs=pl.BlockSpec((1, 8, BRANCH_WIDTH), lambda b, pt: (b, 0, 0)),
        scratch_shapes=[pltpu.VMEM((page, width), F32)],
    )
    out = pl.pallas_call(
        functools.partial(_sb_sample_kernel, n_pages=n_pages, nd=nd),
        grid_spec=grid_spec,
        out_shape=jax.ShapeDtypeStruct((ns, 8, BRANCH_WIDTH), F32),
        compiler_params=_params("arbitrary"),
        name="sb_sample",
    )(page_table, q16, new, _tri(page, upper=False), *([cache] * n_pages))
    return out[:, :nd]


def _compress_math(x_ref, nc, pe_ref, w_ref, kg, bd1):
    outs = []
    for kv in range(2):
        acc = jnp.zeros((nc, NSA_KV_WIDTH), F32)
        for l in range(CMP_BLOCK):
            xl = x_ref[kv, pl.ds(l, nc, stride=CMP_BLOCK), :] + pe_ref[kv, l:l + 1, :]
            acc = acc + _dot(xl.astype(BF16), w_ref[kv, l])
        outs.append(acc)
    return _head_rms(outs[0], bd1, kg), outs[1]


def _compress_prompt_kernel(x_ref, pe_ref, w_ref, kg_ref, bd_ref, o_ref, xs_ref):
    xs_ref[0] = x_ref[0, :, :NSA_KV_WIDTH]
    xs_ref[1] = x_ref[0, :, NSA_KV_WIDTH:]
    k, v = _compress_math(xs_ref, o_ref.shape[1], pe_ref, w_ref, kg_ref[...], bd_ref[...])
    o_ref[0, :, :NSA_KV_WIDTH] = k
    o_ref[0, :, NSA_KV_WIDTH:] = v


def _compress_prompt(x, pe, w, kg, bd1):
    b, t, _ = x.shape
    nc = t // CMP_BLOCK
    return pl.pallas_call(
        _compress_prompt_kernel,
        grid=(b,),
        in_specs=[pl.BlockSpec((1, t, 2 * NSA_KV_WIDTH), lambda i: (i, 0, 0)), _full(pe.shape), _full(w.shape),
                  _full(kg.shape), _full(bd1.shape)],
        out_specs=pl.BlockSpec((1, nc, 2 * NSA_KV_WIDTH), lambda i: (i, 0, 0)),
        out_shape=jax.ShapeDtypeStruct((b, nc, 2 * NSA_KV_WIDTH), F32),
        scratch_shapes=[pltpu.VMEM((2, t, NSA_KV_WIDTH), F32)],
        compiler_params=_params("parallel"),
        name="compress_prompt",
    )(x, pe, w, kg, bd1)


def _compress_sample_kernel(pt_ref, pe_ref, w_ref, kg_ref, bd_ref, *rest, n_pages):
    pages, (o_ref, x_ref) = rest[:n_pages], rest[n_pages:]
    page = pages[0].shape[1]
    for p in range(n_pages):
        for kv in range(2):
            x_ref[kv, p * page:(p + 1) * page, :] = pages[p][kv * NSA_KV_WIDTH:(kv + 1) * NSA_KV_WIDTH, :].T
    k, v = _compress_math(x_ref, o_ref.shape[1], pe_ref, w_ref, kg_ref[...], bd_ref[...])
    o_ref[0, :, :NSA_KV_WIDTH] = k
    o_ref[0, :, NSA_KV_WIDTH:] = v


def _compress_sample(page_table, cache, layer, pe, w, kg, bd1):
    ns, n_pages = page_table.shape
    width, page = cache.shape[2:]
    nc = n_pages * page // CMP_BLOCK
    const = lambda shape: pl.BlockSpec(shape, lambda b, pt, n=len(shape): (0,) * n)
    page_specs = [pl.BlockSpec((None, None, width, page), lambda b, pt, p=p: (layer, pt[b, p], 0, 0))
                  for p in range(n_pages)]
    grid_spec = pltpu.PrefetchScalarGridSpec(
        num_scalar_prefetch=1,
        grid=(ns,),
        in_specs=[const(pe.shape), const(w.shape), const(kg.shape), const(bd1.shape)] + page_specs,
        out_specs=pl.BlockSpec((1, nc, 2 * NSA_KV_WIDTH), lambda b, pt: (b, 0, 0)),
        scratch_shapes=[pltpu.VMEM((2, n_pages * page, NSA_KV_WIDTH), F32)],
    )
    return pl.pallas_call(
        functools.partial(_compress_sample_kernel, n_pages=n_pages),
        grid_spec=grid_spec,
        out_shape=jax.ShapeDtypeStruct((ns, nc, 2 * NSA_KV_WIDTH), F32),
        compiler_params=_params("arbitrary"),
        name="compress_sample",
    )(page_table, pe, w, kg, bd1, *([cache] * n_pages))


def _flash_t(q_t, k_ref, vt_ref, acc_ref, lo, hi, tk, mask_fn):
    r = q_t.shape[1]
    acc_ref[...] = jnp.zeros_like(acc_ref)

    def body(j, carry):
        m, l = carry
        kpos = j * tk + lax.broadcasted_iota(jnp.int32, (tk, 1), 0)
        s = jnp.where(mask_fn(j, kpos), _dot(k_ref[j], q_t), NEG)
        m2 = jnp.maximum(m, jnp.max(s, axis=0, keepdims=True))
        a = jnp.exp(m - m2)
        p = jnp.exp(s - m2)
        acc_ref[...] = a * acc_ref[...] + _dot(vt_ref[j], p.astype(BF16))
        return m2, a * l + jnp.sum(p, axis=0, keepdims=True)

    _, l = lax.fori_loop(lo, hi, body, (jnp.full((1, r), NEG, F32), jnp.zeros((1, r), F32)))
    return acc_ref[...] / l


def _rank_select(score, n_sel):
    nb = score.shape[0]
    row = lax.broadcasted_iota(jnp.int32, (nb, 1), 0)
    ahead = jnp.zeros(score.shape, F32)
    for m in range(nb):
        sm = score[m:m + 1, :]
        ahead = ahead + jnp.where((sm > score) | ((sm == score) & (row > m)), 1.0, 0.0)
    return (ahead < n_sel).astype(F32)


def _block_scores(pair, tvec):
    n = lax.broadcasted_iota(jnp.int32, (pair.shape[0], 1), 0)
    cur = _div(tvec, SEL_BLOCK)
    forced = (n == 0) | (n == cur) | (n == cur - 1)
    return jnp.where(forced, 1e30, jnp.where(n * SEL_BLOCK <= tvec, pair, -1.0))


def _nsa_prompt_kernel(qn_ref, qr_ref, gate_ref, kc_ref, sel_ref, win_ref, o_ref,
                       selk_ref, selvt_ref, wink_ref, winvt_ref, kck_ref, kcvt_ref, imp_ref, sel8_ref, acc_ref,
                       *, tq, tk, n_sel, ns):
    i = pl.program_id(1)
    n_all = selk_ref.shape[0]
    nc = kc_ref.shape[1]
    blocks_per_tile = tk // SEL_BLOCK

    @pl.when(i == 0)
    def _():
        for src, k_ref, vt_ref in ((sel_ref, selk_ref, selvt_ref), (win_ref, wink_ref, winvt_ref)):
            for j in range(n_all):
                tile = src[0, j * tk:(j + 1) * tk, :]
                k_ref[j] = tile[:, :NSA_KV_WIDTH].astype(BF16)
                vt_ref[j] = tile[:, NSA_KV_WIDTH:].T.astype(BF16)
        kc = kc_ref[0]
        pad = jnp.zeros((LANES - nc, NSA_KV_WIDTH), F32)
        kck_ref[...] = jnp.concatenate([kc[:, :NSA_KV_WIDTH], pad], axis=0).astype(BF16)
        kcvt_ref[...] = jnp.concatenate([kc[:, NSA_KV_WIDTH:], pad], axis=0).T.astype(BF16)
        sel8_ref[...] = jnp.zeros_like(sel8_ref)

    t0 = i * tq
    feat = lax.broadcasted_iota(jnp.int32, (LANES, 1), 0)
    tlane = lax.broadcasted_iota(jnp.int32, (1, tq), 1)
    tvec = t0 + jnp.concatenate([tlane] * 4, axis=1)
    tvec2 = t0 + jnp.concatenate([tlane] * 2, axis=1)

    def stack_t(q):
        q_feat = q.T * SCALE
        parts = [jnp.where(_div(feat, HEAD_DIM) == g, q_feat[j * LANES:(j + 1) * LANES], 0.0)
                 for g in (0, 1) for j in (0, 1)]
        return jnp.concatenate(parts, axis=1).astype(BF16)

    qn_t, qr_t = stack_t(qn_ref[0]), stack_t(qr_ref[0])

    c = lax.broadcasted_iota(jnp.int32, (LANES, 1), 0)
    cvalid = (((c + 1) * CMP_BLOCK - 1) <= tvec) & (c < nc)
    s = jnp.where(cvalid, _dot(kck_ref[...], qn_t), NEG)
    e = jnp.exp(s - jnp.max(s, axis=0, keepdims=True))
    any_valid = jnp.max(cvalid.astype(F32), axis=0, keepdims=True)
    p = e / jnp.sum(e, axis=0, keepdims=True) * any_valid
    o_cmp = _dot(kcvt_ref[...], p.astype(BF16))

    imp_ref[0] = p[:, 0:tq] + p[:, tq:2 * tq]
    imp_ref[1] = p[:, 2 * tq:3 * tq] + p[:, 3 * tq:4 * tq]
    nb = sel8_ref.shape[0] * blocks_per_tile
    pair = jnp.concatenate([imp_ref[g, pl.ds(0, nb, stride=2), :] + imp_ref[g, pl.ds(1, nb, stride=2), :]
                            for g in (0, 1)], axis=1)
    sel = _rank_select(_block_scores(pair, tvec2), n_sel)
    for j in range(n_all):
        sel8_ref[j, 0:blocks_per_tile, :] = sel[j * blocks_per_tile:(j + 1) * blocks_per_tile, :]

    hi = (t0 + tq - 1) // tk + 1
    krow = lax.broadcasted_iota(jnp.int32, (tk, 1), 0)

    def sel_mask(j, kpos):
        s8 = sel8_ref[j]
        chosen = s8[0:1, :]
        for bi in range(1, blocks_per_tile):
            chosen = jnp.where(krow >= bi * SEL_BLOCK, s8[bi:bi + 1, :], chosen)
        chosen4 = jnp.concatenate([chosen[:, :tq], chosen[:, :tq], chosen[:, tq:], chosen[:, tq:]], axis=1)
        return (chosen4 > 0.5) & (kpos <= tvec)

    o_sel = _flash_t(qr_t, selk_ref, selvt_ref, acc_ref, 0, hi, tk, sel_mask)
    lo = jnp.maximum(t0 - WINDOW + 1, 0) // tk
    o_win = _flash_t(qr_t, wink_ref, winvt_ref, acc_ref, lo, hi, tk,
                     lambda j, kpos: (kpos <= tvec) & (kpos > tvec - WINDOW))

    gate_t = gate_ref[0].T
    halves = []
    for j in (0, 1):
        y = 0.0
        for br, o in enumerate((o_cmp, o_sel, o_win)):
            picked = jnp.concatenate([o[g * HEAD_DIM:(g + 1) * HEAD_DIM, (2 * g + j) * tq:(2 * g + j + 1) * tq]
                                      for g in (0, 1)], axis=0)
            y = y + gate_t[br * BRANCH_WIDTH + j * LANES: br * BRANCH_WIDTH + (j + 1) * LANES] * picked
        halves.append(y)
    o_ref[0] = jnp.concatenate(halves, axis=0).T


def _nsa_prompt(qn, qr, gate, kc, sel, win):
    b, t, _ = qn.shape
    tq = tk = min(NSA_TK, t)
    assert tk % SEL_BLOCK == 0
    ns = t // SEL_BLOCK
    n_sel = min(N_SELECT, ns)
    n_tiles = t // tk
    qspec = lambda w: pl.BlockSpec((1, tq, w), lambda bi, i: (bi, i, 0))
    seq = lambda n, w: pl.BlockSpec((1, n, w), lambda bi, i: (bi, 0, 0))
    tiles = lambda r, c: pltpu.VMEM((n_tiles, r, c), BF16)
    return pl.pallas_call(
        functools.partial(_nsa_prompt_kernel, tq=tq, tk=tk, n_sel=n_sel, ns=ns),
        grid=(b, t // tq),
        in_specs=[qspec(256), qspec(256), qspec(768), seq(kc.shape[1], 256), seq(t, 256), seq(t, 256)],
        out_specs=qspec(256),
        out_shape=jax.ShapeDtypeStruct((b, t, BRANCH_WIDTH), F32),
        scratch_shapes=[tiles(tk, NSA_KV_WIDTH), tiles(NSA_KV_WIDTH, tk), tiles(tk, NSA_KV_WIDTH),
                        tiles(NSA_KV_WIDTH, tk),
                        pltpu.VMEM((LANES, NSA_KV_WIDTH), BF16), pltpu.VMEM((NSA_KV_WIDTH, LANES), BF16),
                        pltpu.VMEM((2, LANES, tq), F32), pltpu.VMEM((n_tiles, SUBLANES, 2 * tq), F32),
                        pltpu.VMEM((NSA_KV_WIDTH, 4 * tq), F32)],
        compiler_params=_params("parallel", "arbitrary"),
        name="nsa_prompt",
    )(qn, qr, gate, kc, sel, win)


def _compressed_attend(qn, kc, tvec):
    nc = kc.shape[0]
    pad = jnp.zeros((LANES - nc, NSA_KV_WIDTH), F32)
    kck = jnp.concatenate([kc[:, :NSA_KV_WIDTH], pad], axis=0).astype(BF16)
    kcv = jnp.concatenate([kc[:, NSA_KV_WIDTH:], pad], axis=0).astype(BF16)
    lane = lax.broadcasted_iota(jnp.int32, (1, LANES), 1)
    valid = (((lane + 1) * CMP_BLOCK - 1) <= tvec) & (lane < nc)
    s = jnp.where(valid, _dot_nt(qn, kck), NEG)
    e = jnp.exp(s - jnp.max(s, axis=-1, keepdims=True))
    any_valid = jnp.max(valid.astype(F32), axis=-1, keepdims=True)
    p = e / jnp.sum(e, axis=-1, keepdims=True) * any_valid
    return p, _dot(p.astype(BF16), kcv)


def _select_blocks(pair, tvec, n_sel):
    lane = lax.broadcasted_iota(jnp.int32, (1, LANES), 1)
    even = (lane & 1) == 0
    cur2 = _div(tvec, SEL_BLOCK) * 2
    forced = even & ((lane == 0) | (lane == cur2) | (lane == cur2 - 2))
    valid = even & (lane * (SEL_BLOCK // 2) <= tvec)
    score = jnp.where(forced, 1e30, jnp.where(valid, pair, jnp.where(even, -1.0, -2.0)))
    lanef = lane.astype(F32)
    sel = jnp.zeros(pair.shape, F32)
    for _ in range(n_sel):
        best = jnp.max(score, axis=-1, keepdims=True)
        idx = jnp.min(jnp.where(score == best, lanef, 1e9), axis=-1, keepdims=True)
        hit = lanef == idx
        sel = jnp.where(hit, 1.0, sel)
        score = jnp.where(hit, -3.0, score)
    return sel


def _softmax_tiles(q, tiles):
    scores = [jnp.where(msk, _dot(q, kt), NEG) for kt, _, msk in tiles]
    m = functools.reduce(jnp.maximum, [jnp.max(s, axis=-1, keepdims=True) for s in scores])
    es = [jnp.exp(s - m) for s in scores]
    l = sum(jnp.sum(e, axis=-1, keepdims=True) for e in es)
    o = sum(_dot_nt(e.astype(BF16), vt) for e, (_, vt, _) in zip(es, tiles))
    return o / l


def _nsa_sample_kernel(pt_ref, qn_ref, qr_ref, gate_ref, kc_ref, seln_ref, win_ref, winn_ref, e_ref, *rest,
                       n_pages, nd, past, n_sel):
    pages, (o_ref, wino_ref, newt_ref) = rest[:n_pages], rest[n_pages:]

    @pl.when(pl.program_id(0) == 0)
    def _():
        newt_ref[...] = jnp.zeros_like(newt_ref)

    newt_ref[0, 0:nd, :] = seln_ref[0]
    newt_ref[1, 0:nd, :] = winn_ref[0]
    rows = 4 * nd
    row = lax.broadcasted_iota(jnp.int32, (rows, 1), 0)
    lane = lax.broadcasted_iota(jnp.int32, (1, LANES), 1)
    group_mask = (_div(lane, HEAD_DIM) == _div(row, 2 * nd)).astype(F32)
    tau = _mod(row, nd)
    tvec = past + tau
    qn = (qn_ref[0] * SCALE * group_mask).astype(BF16)
    qr = (qr_ref[0] * SCALE * group_mask).astype(BF16)

    p, o_cmp = _compressed_attend(qn, kc_ref[0], tvec)
    imp = p + pltpu.roll(p, nd, 0)
    pair = imp + pltpu.roll(imp, LANES - 1, 1)
    sel = _select_blocks(pair, tvec, n_sel)
    sel4 = jnp.where(_mod(_div(row, nd), 2) == 1, sel, pltpu.roll(sel, rows - nd, 0)).astype(BF16)

    page = pages[0].shape[1]
    assert page == LANES
    tiles = []
    for pi in range(n_pages):
        pg = pages[pi][...].astype(BF16)
        tiles.append((pg[:NSA_KV_WIDTH], pg[NSA_KV_WIDTH:], _dot(sel4, e_ref[pi]) > 0.5))
    new_sel = newt_ref[0].T.astype(BF16)
    tiles.append((new_sel[:NSA_KV_WIDTH], new_sel[NSA_KV_WIDTH:],
                  (_dot(sel4, e_ref[n_pages]) > 0.5) & (lane <= tau)))
    o_sel = _softmax_tiles(qr, tiles)

    wb = win_ref.shape[2]
    tiles = []
    for wi in range(wb // page):
        wt = win_ref[0, :, wi * page:(wi + 1) * page].astype(BF16)
        r = wi * page + lane
        tiles.append((wt[:NSA_KV_WIDTH], wt[NSA_KV_WIDTH:], (r > tau + (wb - WINDOW)) & (r >= wb - past)))
    new_win_t = newt_ref[1].T
    new_win = new_win_t.astype(BF16)
    tiles.append((new_win[:NSA_KV_WIDTH], new_win[NSA_KV_WIDTH:], lane <= tau))
    o_win = _softmax_tiles(qr, tiles)

    y = (gate_ref[0, 0] * o_cmp + gate_ref[0, 1] * o_sel + gate_ref[0, 2] * o_win) * group_mask
    o_ref[0] = y[0:2 * nd] + y[2 * nd:4 * nd]

    rolled = pltpu.roll(win_ref[0], wb - nd, 1)
    wino_ref[0, :, 0:wb - page] = rolled[:, 0:wb - page]
    wino_ref[0, :, wb - page:wb] = jnp.where(lane >= page - nd, pltpu.roll(new_win_t, page - nd, 1),
                                             rolled[:, wb - page:wb])


def _expand_matrix(n_tiles, tk):
    c = np.arange(LANES)[None, :, None]
    key = (np.arange(n_tiles)[:, None, None] * tk + np.arange(tk)[None, None, :])
    return jnp.asarray((c % 2 == 0) & (c // 2 == key // SEL_BLOCK), BF16)


def _nsa_sample(page_table, qn16, qr16, gate16, kc, sel_new, win_state, win_new, cache_sel, layer, past):
    ns, n_pages = page_table.shape
    nd = sel_new.shape[1]
    assert nd == 4
    width, page = cache_sel.shape[2:]
    wb = win_state.shape[2]
    assert wb % page == 0 and page == LANES
    n_sel = min(N_SELECT, -(-(past + nd) // SEL_BLOCK))
    e = _expand_matrix(n_pages + 1, page)
    per_seq = lambda *s: pl.BlockSpec((1,) + s, lambda b, pt, n=len(s): (b,) + (0,) * n)
    page_specs = [pl.BlockSpec((None, None, width, page), lambda b, pt, p=p: (layer, pt[b, p], 0, 0))
                  for p in range(n_pages)]
    grid_spec = pltpu.PrefetchScalarGridSpec(
        num_scalar_prefetch=1,
        grid=(ns,),
        in_specs=[per_seq(16, LANES), per_seq(16, LANES), per_seq(3, 16, LANES), per_seq(kc.shape[1], 256),
                  per_seq(nd, 256), per_seq(256, wb), per_seq(nd, 256),
                  pl.BlockSpec(e.shape, lambda b, pt: (0, 0, 0))] + page_specs,
        out_specs=[per_seq(2 * nd, LANES), per_seq(256, wb)],
        scratch_shapes=[pltpu.VMEM((2, page, 256), F32)],
    )
    return pl.pallas_call(
        functools.partial(_nsa_sample_kernel, n_pages=n_pages, nd=nd, past=past, n_sel=n_sel),
        grid_spec=grid_spec,
        out_shape=[jax.ShapeDtypeStruct((ns, 2 * nd, LANES), F32), jax.ShapeDtypeStruct((ns, 256, wb), F32)],
        compiler_params=_params("arbitrary"),
        name="nsa_sample",
    )(page_table, qn16, qr16, gate16, kc, sel_new, win_state, win_new, e, *([cache_sel] * n_pages))


_SC_HALO, _CC_HALO = 8, 32


def _conv_kernel(u_ref, scb_ref, a_ref, uprev_ref, aprev_ref, stsc_ref, stcc_ref,
                 scw_ref, ccw_ref, ccb_ref, ccg_ref, ccbeta_ref,
                 yc_ref, yd_ref, nsc_ref, ncc_ref, fu_ref, fa_ref, *, tc):
    c = pl.program_id(1)

    @pl.when(c == 0)
    def _():
        fu_ref[:, _SC_HALO - (SC_CONV_LEN - 1):_SC_HALO, :] = stsc_ref[...]
        fa_ref[:, _CC_HALO - (CC_CONV_LEN - 1):_CC_HALO, :] = stcc_ref[...]

    @pl.when(c > 0)
    def _():
        fu_ref[:, 0:_SC_HALO, :] = uprev_ref[...]
        fa_ref[:, 0:_CC_HALO, :] = aprev_ref[...]

    fu_ref[:, _SC_HALO:, :] = u_ref[...]
    fa_ref[:, _CC_HALO:, :] = a_ref[...]
    sub = min(CONV_SUB, tc)
    for s in range(tc // sub):
        r0 = s * sub
        acc = 0.0
        for i in range(SC_CONV_LEN):
            o = r0 + i + _SC_HALO - (SC_CONV_LEN - 1)
            acc = acc + scw_ref[i:i + 1, :] * fu_ref[:, o:o + sub, :]
        yc_ref[:, r0:r0 + sub, :] = scb_ref[:, r0:r0 + sub, :] * acc
        acc = 0.0
        for i in range(CC_CONV_LEN):
            o = r0 + i + _CC_HALO - (CC_CONV_LEN - 1)
            acc = acc + ccw_ref[i:i + 1, :] * fa_ref[:, o:o + sub, :]
        z = acc + ccb_ref[...]
        mu = jnp.mean(z, axis=-1, keepdims=True)
        var = jnp.mean(jnp.square(z - mu), axis=-1, keepdims=True)
        ln = (z - mu) * lax.rsqrt(var + NORM_EPS) * ccg_ref[...] + ccbeta_ref[...]
        yd_ref[:, r0:r0 + sub, :] = ln * jax.nn.sigmoid(ln)
    nsc_ref[...] = fu_ref[:, tc + _SC_HALO - (SC_CONV_LEN - 1):tc + _SC_HALO, :]
    ncc_ref[...] = fa_ref[:, tc + _CC_HALO - (CC_CONV_LEN - 1):tc + _CC_HALO, :]


def _conv(u, scb, a, st_sc, st_cc, scw, ccw, ccb, ccg, ccbeta, nb):
    b, t, w = u.shape
    tc = min(CONV_TC, t)
    chunked = t > tc
    if chunked:
        uprev, aprev = u, a
        prev_u = pl.BlockSpec((nb, _SC_HALO, w), lambda bi, c: (bi, jnp.maximum(c * (tc // _SC_HALO) - 1, 0), 0))
        prev_a = pl.BlockSpec((nb, _CC_HALO, w), lambda bi, c: (bi, jnp.maximum(c * (tc // _CC_HALO) - 1, 0), 0))
    else:
        uprev, aprev = jnp.zeros((b, _SC_HALO, w), F32), jnp.zeros((b, _CC_HALO, w), F32)
        prev_u = pl.BlockSpec((nb, _SC_HALO, w), lambda bi, c: (bi, 0, 0))
        prev_a = pl.BlockSpec((nb, _CC_HALO, w), lambda bi, c: (bi, 0, 0))
    cur = pl.BlockSpec((nb, tc, w), lambda bi, c: (bi, c, 0))
    state = lambda n: pl.BlockSpec((nb, n, w), lambda bi, c: (bi, 0, 0))
    wspec = lambda arr: pl.BlockSpec(arr.shape, lambda bi, c: (0, 0))
    return pl.pallas_call(
        functools.partial(_conv_kernel, tc=tc),
        grid=(b // nb, t // tc),
        in_specs=[cur, cur, cur, prev_u, prev_a, state(SC_CONV_LEN - 1), state(CC_CONV_LEN - 1),
                  wspec(scw), wspec(ccw), wspec(ccb), wspec(ccg), wspec(ccbeta)],
        out_specs=[cur, cur, state(SC_CONV_LEN - 1), state(CC_CONV_LEN - 1)],
        out_shape=[jax.ShapeDtypeStruct((b, t, w), F32), jax.ShapeDtypeStruct((b, t, w), F32),
                   jax.ShapeDtypeStruct((b, SC_CONV_LEN - 1, w), F32),
                   jax.ShapeDtypeStruct((b, CC_CONV_LEN - 1, w), F32)],
        scratch_shapes=[pltpu.VMEM((nb, _SC_HALO + tc, w), F32), pltpu.VMEM((nb, _CC_HALO + tc, w), F32)],
        compiler_params=_params("parallel", "arbitrary"),
        name="conv",
    )(u, scb, a, uprev, aprev, st_sc, st_cc, scw, ccw, ccb, ccg, ccbeta)


def _merge_kernel(x_ref, ya_ref, yb_ref, yc_ref, yd_ref, g_ref, wg_ref, bg_ref, wb_ref, wo_ref, o_ref):
    x = x_ref[...]
    d = x.shape[-1]
    h = _rms(x, g_ref[...]).astype(BF16)
    merged = 0.0
    for n, y_ref in enumerate((ya_ref, yb_ref, yc_ref, yd_ref)):
        gate = jax.nn.sigmoid(_dot(h, wg_ref[:, n * d:(n + 1) * d]) + bg_ref[:, n * d:(n + 1) * d])
        merged = merged + gate * _dot(y_ref[...].astype(BF16), wb_ref[n])
    o_ref[...] = x + _dot(merged.astype(BF16), wo_ref[...])


def _merge(x, ya, yb, yc, yd, g, wg, bg, wb, wo):
    n, d = x.shape
    tm = min(TOKEN_TILE, n)
    tok = lambda w: pl.BlockSpec((tm, w), lambda i: (i, 0))
    return pl.pallas_call(
        _merge_kernel,
        grid=(n // tm,),
        in_specs=[tok(d)] + [tok(BRANCH_WIDTH)] * 4 + [_full(a.shape) for a in (g, wg, bg, wb, wo)],
        out_specs=tok(d),
        out_shape=jax.ShapeDtypeStruct((n, d), F32),
        compiler_params=_params("parallel"),
        name="merge",
    )(x, ya, yb, yc, yd, g, wg, bg, wb, wo)


def _w_in_columns():
    splits = (BRANCH_WIDTH,) * 4 + (NSA_KV_WIDTH,) * 6 + (12,) + (BRANCH_WIDTH,) * 5
    offs = np.concatenate([[0], np.cumsum(splits)])
    (o_sbq, o_sbk, _, o_nq, o_ck, _, o_sk, _, o_wk, _, o_ng, o_scb, o_scc, o_sch, o_cca, o_ccg) = offs[:-1]
    r = lambda o, n: np.arange(o, o + n)
    jg_heads = np.concatenate([r((g * 2 + j) * HEAD_DIM, HEAD_DIM) for j in (0, 1) for g in (0, 1)])
    gates = np.concatenate([np.full(HEAD_DIM, o_ng + g * 6 + j * 3 + br)
                            for br in range(3) for j in (0, 1) for g in (0, 1)])
    cols = np.concatenate([r(o_sbq, 256), r(o_sbk, 512), o_nq + jg_heads, r(o_ck, 256), r(o_sk, 256),
                           r(o_wk, 256), gates, r(o_scb, 256), r(o_scc, 256), r(o_sch, 256), r(o_cca, 256),
                           r(o_ccg, 256)])
    assert cols.shape[0] == _P_END
    return cols, jg_heads


def _rope_tables(pos):
    half = HEAD_DIM // 2
    inv_freq = jnp.exp(-math.log(ROPE_THETA) * jnp.arange(half, dtype=F32) / half)
    ang = pos.astype(F32)[:, None] * inv_freq[None, :]
    cos, sin = jnp.cos(ang), jnp.sin(ang)
    reps = BRANCH_WIDTH // HEAD_DIM
    return jnp.tile(jnp.concatenate([cos, cos], -1), (1, reps)), jnp.tile(jnp.concatenate([-sin, sin], -1), (1, reps))


def _rows16(a, nd):
    ns = a.shape[0]
    a = a.reshape(ns, nd, 2, LANES).transpose(0, 2, 1, 3)
    return jnp.broadcast_to(a[:, None], (ns, 2, 2, nd, LANES)).reshape(ns, 4 * nd, LANES)


def _feature_major(cache):
    l, p, rows = cache.shape[:3]
    return cache.transpose(0, 1, 3, 4, 5, 2).reshape(l, p, -1, rows)


def kernel(x_prompt, x_sample, cache_sb_kv, cache_cmp_kv, cache_sel_kv, state_win_kv, state_conv_sc,
           state_conv_cc, page_table, norm_ffn1, w_ffn1_gu, w_ffn1_down, norm_mix, w_in, nsa_q_gain,
           nsa_k_gain, nsa_cmp_pe, nsa_cmp_w, sc_conv_w, cc_conv_w, cc_conv_b, cc_norm_g, cc_norm_b,
           w_branch, w_gate, b_gate, w_out, norm_ffn2, w_ffn2_gu, w_ffn2_down):
    b, t, d = x_prompt.shape
    ns, nd, _ = x_sample.shape
    depth, n_phys, page = cache_sb_kv.shape[:3]
    n_pages = page_table.shape[1]
    past = n_pages * page
    ffn_dim = w_ffn1_down.shape[1]
    n_chunk = ffn_dim // FFN_CHUNK
    wb = state_win_kv.shape[2]

    def gu(w):
        return w.reshape(depth, d, 2, n_chunk, FFN_CHUNK).transpose(0, 2, 3, 1, 4).astype(BF16)

    wgu1, wgu2 = gu(w_ffn1_gu), gu(w_ffn2_gu)
    wd1 = w_ffn1_down.reshape(depth, n_chunk, FFN_CHUNK, d).astype(BF16)
    wd2 = w_ffn2_down.reshape(depth, n_chunk, FFN_CHUNK, d).astype(BF16)
    cols, jg_heads = _w_in_columns()
    w_in_r = w_in[:, :, cols].astype(BF16)
    w_gate_b = w_gate.astype(BF16)
    w_branch_b = w_branch.at[:, 1].set(w_branch[:, 1][:, jg_heads, :]).astype(BF16)
    w_out_b = w_out.astype(BF16)
    eye2 = jnp.eye(2, dtype=F32)
    w_cmp = jnp.einsum("Lklde,gG->LklgdGe", nsa_cmp_w, eye2).reshape(depth, 2, CMP_BLOCK, 128, 128)
    w_cmp = w_cmp.astype(BF16)
    pe_rows = jnp.tile(nsa_cmp_pe, (1, 1, 1, 2))
    row2 = lambda a: a.reshape(depth, 1, -1)
    n1, nm, n2, bg = row2(norm_ffn1), row2(norm_mix), row2(norm_ffn2), row2(b_gate)
    qg = jnp.tile(row2(nsa_q_gain), (1, 1, 4))
    kg = jnp.tile(row2(nsa_k_gain), (1, 1, 2))
    ccb, ccg, ccbeta = row2(cc_conv_b), row2(cc_norm_g), row2(cc_norm_b)
    hd = np.arange(BRANCH_WIDTH) // HEAD_DIM
    bd = jnp.asarray(hd[:, None] == hd[None, :], BF16)
    bd1 = bd[:LANES, :LANES]
    cos_p, sin_p = _rope_tables(jnp.arange(t, dtype=jnp.int32))
    cos_s, sin_s = _rope_tables(past + jnp.arange(ns * nd, dtype=jnp.int32) % nd)

    cache_sb = _feature_major(cache_sb_kv)
    cache_cmp = _feature_major(cache_cmp_kv)
    cache_sel = _feature_major(cache_sel_kv)
    win_state = _feature_major(state_win_kv)

    xp = x_prompt.reshape(b * t, d)
    xs = x_sample.reshape(ns * nd, d)
    zeros_sc = jnp.zeros((b, SC_CONV_LEN - 1, BRANCH_WIDTH), F32)
    zeros_cc = jnp.zeros((b, CC_CONV_LEN - 1, BRANCH_WIDTH), F32)
    outs_p = [[] for _ in range(6)]
    outs_s = [[] for _ in range(6)]
    for l in range(depth):
        xp = _ffn(xp, n1[l], wgu1[l], wd1[l])
        xs = _ffn(xs, n1[l], wgu1[l], wd1[l])

        (sbq, sbkv, qn, qr, cmp_, sel, win, gate, scb, scu, cca) = _proj(
            xp, nm[l], w_in_r[l], cos_p, sin_p, qg[l], kg[l], bd)
        seq = lambda a: a.reshape(b, t, a.shape[-1])
        ya = _sb_prompt(seq(sbq), seq(sbkv))
        kc = _compress_prompt(seq(cmp_), pe_rows[l], w_cmp[l], kg[l], bd1)
        yb = _nsa_prompt(seq(qn), seq(qr), seq(gate), kc, seq(sel), seq(win))
        yc, yd, nsc, ncc = _conv(seq(scu), seq(scb), seq(cca), zeros_sc, zeros_cc, sc_conv_w[l], cc_conv_w[l],
                                 ccb[l], ccg[l], ccbeta[l], nb=1)
        flat = lambda a: a.reshape(b * t, a.shape[-1])
        xp = _merge(xp, flat(ya), flat(yb), flat(yc), flat(yd), nm[l], w_gate_b[l], bg[l], w_branch_b[l],
                    w_out_b[l])
        for lst, a in zip(outs_p, (seq(sbkv), seq(cmp_), seq(sel), seq(win)[:, t - min(WINDOW, t):], nsc, ncc)):
            lst.append(a)

        (sbq, sbkv, qn, qr, cmp_, sel, win, gate, scb, scu, cca) = _proj(
            xs, nm[l], w_in_r[l], cos_s, sin_s, qg[l], kg[l], bd)
        seq = lambda a: a.reshape(ns, nd, a.shape[-1])
        q16 = jnp.tile(seq(sbq), (1, 4, 1))
        ya = _sb_sample(page_table, q16, seq(sbkv), cache_sb, l)
        kc = _compress_sample(page_table, cache_cmp, l, pe_rows[l], w_cmp[l], kg[l], bd1)
        gate16 = jnp.stack([_rows16(seq(gate)[:, :, br * 256:(br + 1) * 256], nd) for br in range(3)], axis=1)
        y8, win_out = _nsa_sample(page_table, _rows16(seq(qn), nd), _rows16(seq(qr), nd), gate16, kc,
                                  seq(sel), win_state[l], seq(win), cache_sel, l, past)
        yb = y8.reshape(ns, 2, nd, LANES).transpose(0, 2, 1, 3).reshape(ns, nd, BRANCH_WIDTH)
        yc, yd, nsc, ncc = _conv(seq(scu), seq(scb), seq(cca), state_conv_sc[l], state_conv_cc[l],
                                 sc_conv_w[l], cc_conv_w[l], ccb[l], ccg[l], ccbeta[l], nb=min(8, ns))
        flat = lambda a: a.reshape(ns * nd, a.shape[-1])
        xs = _merge(xs, flat(ya), flat(yb), flat(yc), flat(yd), nm[l], w_gate_b[l], bg[l], w_branch_b[l],
                    w_out_b[l])
        for lst, a in zip(outs_s, (seq(sbkv), seq(cmp_), seq(sel), win_out, nsc, ncc)):
            lst.append(a)

        xp = _ffn(xp, n2[l], wgu2[l], wd2[l])
        xs = _ffn(xs, n2[l], wgu2[l], wd2[l])

    def kv5(a, heads):
        a = jnp.stack(a)
        return a.reshape(a.shape[:3] + (2, heads, HEAD_DIM))

    sb_heads, nsa_kv_heads = BRANCH_WIDTH // HEAD_DIM, NSA_KV_WIDTH // HEAD_DIM
    win_s = jnp.stack(outs_s[3])
    win_s = win_s.reshape(depth, ns, 2, nsa_kv_heads, HEAD_DIM, wb).transpose(0, 1, 5, 2, 3, 4)
    return (xp.reshape(b, t, d), xs.reshape(ns, nd, d),
            kv5(outs_p[0], sb_heads), kv5(outs_s[0], sb_heads),
            kv5(outs_p[1], nsa_kv_heads), kv5(outs_s[1], nsa_kv_heads),
            kv5(outs_p[2], nsa_kv_heads), kv5(outs_s[2], nsa_kv_heads),
            kv5(outs_p[3], nsa_kv_heads), win_s,
            jnp.stack(outs_p[4]), jnp.stack(outs_s[4]), jnp.stack(outs_p[5]), jnp.stack(outs_s[5]))
```

```python
import functools
import math

import numpy as np
import jax
import jax.numpy as jnp
from jax import lax
from jax.experimental import pallas as pl
from jax.experimental.pallas import tpu as pltpu

F32, BF16 = jnp.float32, jnp.bfloat16

HEAD_DIM = 64
BRANCH_WIDTH = 256
N_BRANCH = 4
NSA_KV_WIDTH = 128
CMP_BLOCK = 32
SEL_BLOCK = 64
N_SELECT = 16
WINDOW = 512
SC_CONV_LEN = 3
CC_CONV_LEN = 31
ROPE_THETA = 10000.0
NORM_EPS = 1e-6
NEG = -1e30
SCALE = HEAD_DIM ** -0.5

LANES = 128
SUBLANES = 8
VMEM_LIMIT_BYTES = 56 * 1024 * 1024
TOKEN_TILE = 512
FFN_CHUNK = 256
ATT_TQ = 128
SB_TK = 256
NSA_TK = 256
SAMPLE_SEQS_PER_STEP = 4
CMP_PITCH = CMP_BLOCK + 1
CONV_TC = 256
CONV_SUB = 64

_NT = (((1,), (1,)), ((), ()))


def _params(*sem):
    return pltpu.CompilerParams(dimension_semantics=sem, vmem_limit_bytes=VMEM_LIMIT_BYTES)


def _full(shape):
    n = len(shape)
    return pl.BlockSpec(shape, lambda *_, n=n: (0,) * n)


def _dot(a, b):
    return jnp.dot(a, b, preferred_element_type=F32)


def _dot_nt(a, b):
    return lax.dot_general(a, b, _NT, preferred_element_type=F32)


def _div(x, n):
    assert n & (n - 1) == 0
    return x >> (n.bit_length() - 1)


def _mod(x, n):
    assert n & (n - 1) == 0
    return x & (n - 1)


def _rms(x, g):
    return x * lax.rsqrt(jnp.mean(x * x, axis=-1, keepdims=True) + NORM_EPS) * g


def _split(x):
    hi = x.astype(BF16)
    return hi, (x - hi.astype(F32)).astype(BF16)


def _split_dot(x, m):
    hi, lo = _split(x)
    return _dot(hi, m) + _dot(lo, m)


def _head_rms(x, bd, g):
    ms = _split_dot(x * x, bd) * (1.0 / HEAD_DIM)
    return x * lax.rsqrt(ms + NORM_EPS) * g


def _rope(x, cos, sin_signed):
    w = x.shape[-1]
    lane = lax.broadcasted_iota(jnp.int32, x.shape, 1)
    first_half = (lane & (HEAD_DIM - 1)) < HEAD_DIM // 2
    rot = jnp.where(first_half, pltpu.roll(x, w - HEAD_DIM // 2, 1), pltpu.roll(x, HEAD_DIM // 2, 1))
    return x * cos + rot * sin_signed


def _ffn_kernel(x_ref, g_ref, wgu_ref, wd_ref, o_ref, h_ref, acc_ref):
    h_ref[...] = _rms(x_ref[...], g_ref[...]).astype(BF16)
    acc_ref[...] = jnp.zeros_like(acc_ref)

    def body(c, carry):
        h = h_ref[...]
        g = _dot(h, wgu_ref[0, c])
        u = _dot(h, wgu_ref[1, c])
        a = (g * jax.nn.sigmoid(g) * u).astype(BF16)
        acc_ref[...] += _dot(a, wd_ref[c])
        return carry

    lax.fori_loop(0, wd_ref.shape[0], body, 0)
    o_ref[...] = x_ref[...] + 0.5 * acc_ref[...]


def _ffn(x, g, wgu, wd):
    n, d = x.shape
    tm = min(TOKEN_TILE, n)
    return pl.pallas_call(
        _ffn_kernel,
        grid=(n // tm,),
        in_specs=[pl.BlockSpec((tm, d), lambda i: (i, 0)), _full(g.shape), _full(wgu.shape), _full(wd.shape)],
        out_specs=pl.BlockSpec((tm, d), lambda i: (i, 0)),
        out_shape=jax.ShapeDtypeStruct((n, d), F32),
        scratch_shapes=[pltpu.VMEM((tm, d), BF16), pltpu.VMEM((tm, d), F32)],
        compiler_params=_params("parallel"),
        name="ffn",
    )(x, g, wgu, wd)


_P_SBQ, _P_SBKV, _P_NQ, _P_CMP, _P_SEL, _P_WIN, _P_GATE = 0, 256, 768, 1024, 1280, 1536, 1792
_P_SCB, _P_SCC, _P_SCH, _P_CCA, _P_CCG, _P_END = 2560, 2816, 3072, 3328, 3584, 3840


def _proj_kernel(x_ref, g_ref, w_ref, cos_ref, sin_ref, qg_ref, kg_ref, bd_ref,
                 sbq_ref, sbkv_ref, qn_ref, qr_ref, cmp_ref, sel_ref, win_ref, gate_ref,
                 scb_ref, scu_ref, cca_ref):
    h = _rms(x_ref[...], g_ref[...]).astype(BF16)

    def mm(a, b):
        return _dot(h, w_ref[:, a:b])

    cos, sin, bd = cos_ref[...], sin_ref[...], bd_ref[...]
    sbq_ref[...] = mm(_P_SBQ, _P_SBKV)
    sbkv_ref[...] = mm(_P_SBKV, _P_NQ)
    qn = _head_rms(mm(_P_NQ, _P_CMP), bd, qg_ref[...])
    qn_ref[...] = qn
    qr_ref[...] = _rope(qn, cos, sin)
    cmp_ref[...] = mm(_P_CMP, _P_SEL)
    bd1, cos1, sin1 = bd[:LANES, :LANES], cos[:, :LANES], sin[:, :LANES]
    for a, ref in ((_P_SEL, sel_ref), (_P_WIN, win_ref)):
        kv = mm(a, a + 2 * NSA_KV_WIDTH)
        ref[:, :NSA_KV_WIDTH] = _rope(_head_rms(kv[:, :NSA_KV_WIDTH], bd1, kg_ref[...]), cos1, sin1)
        ref[:, NSA_KV_WIDTH:] = kv[:, NSA_KV_WIDTH:]
    gate_ref[...] = jax.nn.sigmoid(mm(_P_GATE, _P_SCB))
    scb_ref[...] = mm(_P_SCB, _P_SCC)
    scu_ref[...] = mm(_P_SCC, _P_SCH) * mm(_P_SCH, _P_CCA)
    cca_ref[...] = mm(_P_CCA, _P_CCG) * jax.nn.sigmoid(mm(_P_CCG, _P_END))


def _proj(x, g, w, cos, sin, qg, kg, bd):
    n, d = x.shape
    tm = min(TOKEN_TILE, n)
    n_rope_tiles = cos.shape[0] // tm
    widths = (256, 512, 256, 256, 256, 256, 256, 768, 256, 256, 256)
    tok = lambda wd_: pl.BlockSpec((tm, wd_), lambda i: (i, 0))
    rope = pl.BlockSpec((tm, cos.shape[1]), lambda i: (i % n_rope_tiles, 0))
    return pl.pallas_call(
        _proj_kernel,
        grid=(n // tm,),
        in_specs=[tok(d), _full(g.shape), _full(w.shape), rope, rope, _full(qg.shape), _full(kg.shape),
                  _full(bd.shape)],
        out_specs=[tok(wd_) for wd_ in widths],
        out_shape=[jax.ShapeDtypeStruct((n, wd_), F32) for wd_ in widths],
        compiler_params=_params("parallel"),
        name="proj",
    )(x, g, w, cos, sin, qg, kg, bd)


def _sb_logs(z):
    sp = jnp.log1p(jnp.exp(-jnp.abs(z)))
    return jnp.minimum(z, 0.0) - sp, -jnp.maximum(z, 0.0) - sp


def _sb_tile_t(k, vts, q_ts, valid, carry, tri_t):
    logs = [_sb_logs(_dot(k, q)) for q in q_ts]
    keeps = [lk if valid is None else jnp.where(valid, lk, 0.0) for _, lk in logs]
    between = [_dot(tri_t, hi) + _dot(tri_t, lo) for hi, lo in [_split(lk) for lk in keeps]]
    ws = [jnp.exp(lb + bt + c) for (lb, _), bt, (c, _) in zip(logs, between, carry)]
    if valid is not None:
        ws = [jnp.where(valid, w, 0.0) for w in ws]
    pvs = [_dot(vt, w.astype(BF16)) for vt, w in zip(vts, ws)]
    return tuple((c + jnp.sum(lk, axis=0, keepdims=True), acc + pv)
                 for (c, acc), lk, pv in zip(carry, keeps, pvs))


def _sb_prompt_kernel(q_ref, kv_ref, tri_ref, o_ref, kb_ref, vtb_ref, *, tq, tk):
    i = pl.program_id(1)
    n_all = kb_ref.shape[0]

    @pl.when(i == 0)
    def _():
        for j in range(n_all):
            tile = kv_ref[0, j * tk:(j + 1) * tk, :]
            kb_ref[j] = tile[:, :BRANCH_WIDTH].astype(BF16)
            vtb_ref[j] = tile[:, BRANCH_WIDTH:].T.astype(BF16)

    t0 = i * tq
    heads = BRANCH_WIDTH // HEAD_DIM
    feat = lax.broadcasted_iota(jnp.int32, (BRANCH_WIDTH, 1), 0)
    q_feat = q_ref[0].T * SCALE
    q_h = [jnp.where(_div(feat, HEAD_DIM) == h, q_feat, 0.0).astype(BF16) for h in range(heads)]
    tvec = t0 + lax.broadcasted_iota(jnp.int32, (1, tq), 1)
    tri_t = tri_ref[...]
    j_diag = (t0 + tq - 1) // tk
    kpos = j_diag * tk + lax.broadcasted_iota(jnp.int32, (tk, 1), 0)

    def tile(j, carry, valid):
        vts = [vtb_ref[j, h * HEAD_DIM:(h + 1) * HEAD_DIM, :] for h in range(heads)]
        return _sb_tile_t(kb_ref[j], vts, q_h, valid, carry, tri_t)

    init = tuple((jnp.zeros((1, tq), F32), jnp.zeros((HEAD_DIM, tq), F32)) for _ in range(heads))
    carry = tile(j_diag, init, kpos < tvec)
    carry = lax.fori_loop(0, j_diag, lambda it, carry: tile(j_diag - 1 - it, carry, None), carry)
    o_ref[0] = jnp.concatenate([acc for _, acc in carry], axis=0).T


def _tri(n, upper):
    r = np.arange(n)
    m = r[:, None] < r[None, :] if upper else r[:, None] > r[None, :]
    return jnp.asarray(m, BF16)


def _sb_prompt(q, kv):
    b, t, _ = q.shape
    tq, tk = min(ATT_TQ, t), min(SB_TK, t)
    assert tk % tq == 0
    return pl.pallas_call(
        functools.partial(_sb_prompt_kernel, tq=tq, tk=tk),
        grid=(b, t // tq),
        in_specs=[pl.BlockSpec((1, tq, BRANCH_WIDTH), lambda bi, i: (bi, i, 0)),
                  pl.BlockSpec((1, t, 2 * BRANCH_WIDTH), lambda bi, i: (bi, 0, 0)),
                  _full((tk, tk))],
        out_specs=pl.BlockSpec((1, tq, BRANCH_WIDTH), lambda bi, i: (bi, i, 0)),
        out_shape=jax.ShapeDtypeStruct((b, t, BRANCH_WIDTH), F32),
        scratch_shapes=[pltpu.VMEM((t // tk, tk, BRANCH_WIDTH), BF16),
                        pltpu.VMEM((t // tk, BRANCH_WIDTH, tk), BF16)],
        compiler_params=_params("parallel", "arbitrary"),
        name="sb_prompt",
    )(q, kv, _tri(tk, upper=True))


def _sb_sample_kernel(pt_ref, q_ref, new_ref, tri_ref, *rest, n_pages, nd, nb):
    pages, (o_ref, newt_ref) = rest[:nb * n_pages], rest[nb * n_pages:]

    @pl.when(pl.program_id(0) == 0)
    def _():
        newt_ref[...] = jnp.zeros_like(newt_ref)

    rows = 4 * nd
    row = lax.broadcasted_iota(jnp.int32, (rows, 1), 0)
    lane = lax.broadcasted_iota(jnp.int32, (1, BRANCH_WIDTH), 1)
    head_mask = (_div(lane, HEAD_DIM) == _div(row, nd)).astype(F32)
    tri = tri_ref[...]
    key = lax.broadcasted_iota(jnp.int32, (1, newt_ref.shape[1]), 1)
    kv_half = lambda x, v: x[v * BRANCH_WIDTH:(v + 1) * BRANCH_WIDTH].astype(BF16)
    qms, tiles = [], []
    for s in range(nb):
        newt_ref[s, 0:nd, :] = new_ref[s]
        qms.append((q_ref[s] * SCALE * head_mask).astype(BF16))
        new_t = newt_ref[s].T
        seq_tiles = [(functools.partial(kv_half, new_t), key < _mod(row, nd))]
        for p in reversed(range(n_pages)):
            page_ref = pages[s * n_pages + p]
            seq_tiles.append((lambda v, r=page_ref: kv_half(r, v), None))
        tiles.append(seq_tiles)
    logs = [[_sb_logs(_dot(qm, get(0))) for get, _ in seq] for qm, seq in zip(qms, tiles)]
    keeps = [[lk if valid is None else jnp.where(valid, lk, 0.0) for (_, lk), (_, valid) in zip(lg, seq)]
             for lg, seq in zip(logs, tiles)]
    between = [[_split_dot(lk, tri) for lk in ks] for ks in keeps]
    weights = []
    for lg, ks, bt, seq in zip(logs, keeps, between, tiles):
        c = jnp.zeros((rows, 1), F32)
        ws = []
        for (lb, _), lk, b, (_, valid) in zip(lg, ks, bt, seq):
            w = jnp.exp(lb + b + c)
            ws.append((w if valid is None else jnp.where(valid, w, 0.0)).astype(BF16))
            c = c + jnp.sum(lk, axis=-1, keepdims=True)
        weights.append(ws)
    for s in range(nb):
        acc = sum(_dot_nt(w, get(1)) for w, (get, _) in zip(weights[s], tiles[s]))
        m = acc * head_mask
        hs = m[0:8] + m[8:16]
        o_ref[s] = hs + pltpu.roll(hs, 4, 0)


def _page_specs(cache, layer, n_pages, nb):
    width, page = cache.shape[2:]
    return [pl.BlockSpec((None, None, width, page), lambda b, pt, s=s, p=p: (layer, pt[b * nb + s, p], 0, 0))
            for s in range(nb) for p in range(n_pages)]


def _sb_sample(page_table, q16, new, cache, layer):
    ns, n_pages = page_table.shape
    nd = new.shape[1]
    assert nd == 4 and q16.shape[1] == 16
    width, page = cache.shape[2:]
    nb = math.gcd(ns, SAMPLE_SEQS_PER_STEP)
    grid_spec = pltpu.PrefetchScalarGridSpec(
        num_scalar_prefetch=1,
        grid=(ns // nb,),
        in_specs=[pl.BlockSpec((nb, 16, BRANCH_WIDTH), lambda b, pt: (b, 0, 0)),
                  pl.BlockSpec((nb, nd, width), lambda b, pt: (b, 0, 0)),
                  pl.BlockSpec((page, page), lambda b, pt: (0, 0))] + _page_specs(cache, layer, n_pages, nb),
        out_specs=pl.BlockSpec((nb, 8, BRANCH_WIDTH), lambda b, pt: (b, 0, 0)),
        scratch_shapes=[pltpu.VMEM((nb, page, width), F32)],
    )
    out = pl.pallas_call(
        functools.partial(_sb_sample_kernel, n_pages=n_pages, nd=nd, nb=nb),
        grid_spec=grid_spec,
        out_shape=jax.ShapeDtypeStruct((ns, 8, BRANCH_WIDTH), F32),
        compiler_params=_params("arbitrary"),
        name="sb_sample",
    )(page_table, q16, new, _tri(page, upper=False), *([cache] * (nb * n_pages)))
    return out[:, :nd]


def _compress_math(x_ref, nc, pitch, pe_ref, w_ref, kg, bd1):
    outs = []
    for kv in range(2):
        acc = jnp.zeros((nc, NSA_KV_WIDTH), F32)
        for l in range(CMP_BLOCK):
            xl = x_ref[kv, pl.ds(l, nc, stride=pitch), :] + pe_ref[kv, l:l + 1, :]
            acc = acc + _dot(xl.astype(BF16), w_ref[kv, l])
        outs.append(acc)
    return _head_rms(outs[0], bd1, kg), outs[1]


def _compress_prompt_kernel(x_ref, pe_ref, w_ref, kg_ref, bd_ref, o_ref, xs_ref):
    xs_ref[0] = x_ref[0, :, :NSA_KV_WIDTH]
    xs_ref[1] = x_ref[0, :, NSA_KV_WIDTH:]
    k, v = _compress_math(xs_ref, o_ref.shape[1], CMP_BLOCK, pe_ref, w_ref, kg_ref[...], bd_ref[...])
    o_ref[0, :, :NSA_KV_WIDTH] = k
    o_ref[0, :, NSA_KV_WIDTH:] = v


def _compress_prompt(x, pe, w, kg, bd1):
    b, t, _ = x.shape
    nc = t // CMP_BLOCK
    return pl.pallas_call(
        _compress_prompt_kernel,
        grid=(b,),
        in_specs=[pl.BlockSpec((1, t, 2 * NSA_KV_WIDTH), lambda i: (i, 0, 0)), _full(pe.shape), _full(w.shape),
                  _full(kg.shape), _full(bd1.shape)],
        out_specs=pl.BlockSpec((1, nc, 2 * NSA_KV_WIDTH), lambda i: (i, 0, 0)),
        out_shape=jax.ShapeDtypeStruct((b, nc, 2 * NSA_KV_WIDTH), F32),
        scratch_shapes=[pltpu.VMEM((2, t, NSA_KV_WIDTH), F32)],
        compiler_params=_params("parallel"),
        name="compress_prompt",
    )(x, pe, w, kg, bd1)


def _compress_sample_kernel(pt_ref, pe_ref, w_ref, kg_ref, bd_ref, *rest, n_pages):
    pages, (o_ref, x_ref) = rest[:n_pages], rest[n_pages:]
    per_page = pages[0].shape[1] // CMP_BLOCK
    for p in range(n_pages):
        for kv in range(2):
            rows = pages[p][kv * NSA_KV_WIDTH:(kv + 1) * NSA_KV_WIDTH, :].T
            for q in range(per_page):
                x_ref[kv, pl.ds((p * per_page + q) * CMP_PITCH, CMP_BLOCK), :] = (
                    rows[q * CMP_BLOCK:(q + 1) * CMP_BLOCK])
    k, v = _compress_math(x_ref, o_ref.shape[1], CMP_PITCH, pe_ref, w_ref, kg_ref[...], bd_ref[...])
    o_ref[0, :, :NSA_KV_WIDTH] = k
    o_ref[0, :, NSA_KV_WIDTH:] = v


def _compress_sample(page_table, cache, layer, pe, w, kg, bd1):
    ns, n_pages = page_table.shape
    width, page = cache.shape[2:]
    nc = n_pages * page // CMP_BLOCK
    const = lambda shape: pl.BlockSpec(shape, lambda b, pt, n=len(shape): (0,) * n)
    page_specs = [pl.BlockSpec((None, None, width, page), lambda b, pt, p=p: (layer, pt[b, p], 0, 0))
                  for p in range(n_pages)]
    grid_spec = pltpu.PrefetchScalarGridSpec(
        num_scalar_prefetch=1,
        grid=(ns,),
        in_specs=[const(pe.shape), const(w.shape), const(kg.shape), const(bd1.shape)] + page_specs,
        out_specs=pl.BlockSpec((1, nc, 2 * NSA_KV_WIDTH), lambda b, pt: (b, 0, 0)),
        scratch_shapes=[pltpu.VMEM((2, -(-nc * CMP_PITCH // SUBLANES) * SUBLANES, NSA_KV_WIDTH), F32)],
    )
    return pl.pallas_call(
        functools.partial(_compress_sample_kernel, n_pages=n_pages),
        grid_spec=grid_spec,
        out_shape=jax.ShapeDtypeStruct((ns, nc, 2 * NSA_KV_WIDTH), F32),
        compiler_params=_params("arbitrary"),
        name="compress_sample",
    )(page_table, pe, w, kg, bd1, *([cache] * n_pages))


def _flash_t(q_ts, k_ref, vt_ref, lo, hi, tk, mask_fn):
    tq = q_ts[0].shape[1]

    def body(j, carry):
        k, vt = k_ref[j], vt_ref[j]
        masks = mask_fn(j, j * tk + lax.broadcasted_iota(jnp.int32, (tk, 1), 0))
        ss = [jnp.where(masks[ci // 2], _dot(k, q), NEG) for ci, q in enumerate(q_ts)]
        m2s = [jnp.maximum(m, jnp.max(s, axis=0, keepdims=True)) for s, (m, _, _) in zip(ss, carry)]
        ps = [jnp.exp(s - m2) for s, m2 in zip(ss, m2s)]
        pvs = [_dot(vt[(ci // 2) * HEAD_DIM:(ci // 2 + 1) * HEAD_DIM], p.astype(BF16)) for ci, p in enumerate(ps)]
        out = []
        for (m, l, acc), m2, p, pv in zip(carry, m2s, ps, pvs):
            a = jnp.exp(m - m2)
            out.append((m2, a * l + jnp.sum(p, axis=0, keepdims=True), a * acc + pv))
        return tuple(out)

    init = tuple((jnp.full((1, tq), NEG, F32), jnp.zeros((1, tq), F32), jnp.zeros((HEAD_DIM, tq), F32))
                 for _ in q_ts)
    return [acc / l for _, l, acc in lax.fori_loop(lo, hi, body, init)]


def _rank_select(score, n_sel):
    nb = score.shape[0]
    row = lax.broadcasted_iota(jnp.int32, (nb, 1), 0)
    ahead = jnp.zeros(score.shape, F32)
    for m in range(nb):
        sm = score[m:m + 1, :]
        ahead = ahead + jnp.where((sm > score) | ((sm == score) & (row > m)), 1.0, 0.0)
    return (ahead < n_sel).astype(F32)


def _block_scores(pair, tvec):
    n = lax.broadcasted_iota(jnp.int32, (pair.shape[0], 1), 0)
    cur = _div(tvec, SEL_BLOCK)
    forced = (n == 0) | (n == cur) | (n == cur - 1)
    return jnp.where(forced, 1e30, jnp.where(n * SEL_BLOCK <= tvec, pair, -1.0))


def _nsa_prompt_kernel(qn_ref, qr_ref, gate_ref, kc_ref, sel_ref, win_ref, o_ref,
                       selk_ref, selvt_ref, wink_ref, winvt_ref, kck_ref, kcvt_ref, imp_ref, sel8_ref,
                       *, tq, tk, n_sel, ns):
    i = pl.program_id(1)
    n_all = selk_ref.shape[0]
    nc = kc_ref.shape[1]
    blocks_per_tile = tk // SEL_BLOCK

    @pl.when(i == 0)
    def _():
        for src, k_ref, vt_ref in ((sel_ref, selk_ref, selvt_ref), (win_ref, wink_ref, winvt_ref)):
            for j in range(n_all):
                tile = src[0, j * tk:(j + 1) * tk, :]
                k_ref[j] = tile[:, :NSA_KV_WIDTH].astype(BF16)
                vt_ref[j] = tile[:, NSA_KV_WIDTH:].T.astype(BF16)
        kc = kc_ref[0]
        pad = jnp.zeros((LANES - nc, NSA_KV_WIDTH), F32)
        kck_ref[...] = jnp.concatenate([kc[:, :NSA_KV_WIDTH], pad], axis=0).astype(BF16)
        kcvt_ref[...] = jnp.concatenate([kc[:, NSA_KV_WIDTH:], pad], axis=0).T.astype(BF16)
        sel8_ref[...] = jnp.zeros_like(sel8_ref)

    t0 = i * tq
    feat = lax.broadcasted_iota(jnp.int32, (LANES, 1), 0)
    tlane = lax.broadcasted_iota(jnp.int32, (1, tq), 1)
    tvec = t0 + jnp.concatenate([tlane] * 4, axis=1)
    tvec2 = t0 + jnp.concatenate([tlane] * 2, axis=1)

    def stack_t(q):
        q_feat = q.T * SCALE
        parts = [jnp.where(_div(feat, HEAD_DIM) == g, q_feat[j * LANES:(j + 1) * LANES], 0.0)
                 for g in (0, 1) for j in (0, 1)]
        return jnp.concatenate(parts, axis=1).astype(BF16)

    qn_t, qr_t = stack_t(qn_ref[0]), stack_t(qr_ref[0])

    c = lax.broadcasted_iota(jnp.int32, (LANES, 1), 0)
    cvalid = (((c + 1) * CMP_BLOCK - 1) <= tvec) & (c < nc)
    s = jnp.where(cvalid, _dot(kck_ref[...], qn_t), NEG)
    e = jnp.exp(s - jnp.max(s, axis=0, keepdims=True))
    any_valid = jnp.max(cvalid.astype(F32), axis=0, keepdims=True)
    p = e / jnp.sum(e, axis=0, keepdims=True) * any_valid
    o_cmp = _dot(kcvt_ref[...], p.astype(BF16))

    imp_ref[0] = p[:, 0:tq] + p[:, tq:2 * tq]
    imp_ref[1] = p[:, 2 * tq:3 * tq] + p[:, 3 * tq:4 * tq]
    nb = sel8_ref.shape[0] * blocks_per_tile
    pair = jnp.concatenate([imp_ref[g, pl.ds(0, nb, stride=2), :] + imp_ref[g, pl.ds(1, nb, stride=2), :]
                            for g in (0, 1)], axis=1)
    sel = _rank_select(_block_scores(pair, tvec2), n_sel)
    for j in range(n_all):
        sel8_ref[j, 0:blocks_per_tile, :] = sel[j * blocks_per_tile:(j + 1) * blocks_per_tile, :]

    hi = (t0 + tq - 1) // tk + 1
    krow = lax.broadcasted_iota(jnp.int32, (tk, 1), 0)

    tq_vec = t0 + tlane

    def sel_mask(j, kpos):
        s8 = sel8_ref[j]
        chosen = s8[0:1, :]
        for bi in range(1, blocks_per_tile):
            chosen = jnp.where(krow >= bi * SEL_BLOCK, s8[bi:bi + 1, :], chosen)
        causal = kpos <= tq_vec
        return [(chosen[:, g * tq:(g + 1) * tq] > 0.5) & causal for g in (0, 1)]

    def win_mask(j, kpos):
        inside = (kpos <= tq_vec) & (kpos > tq_vec - WINDOW)
        return [inside, inside]

    qr_ts = [qr_t[:, ci * tq:(ci + 1) * tq] for ci in range(4)]
    o_sel = _flash_t(qr_ts, selk_ref, selvt_ref, 0, hi, tk, sel_mask)
    lo = jnp.maximum(t0 - WINDOW + 1, 0) // tk
    o_win = _flash_t(qr_ts, wink_ref, winvt_ref, lo, hi, tk, win_mask)

    gate_t = gate_ref[0].T
    parts = []
    for j in (0, 1):
        for g in (0, 1):
            ci = 2 * g + j
            o_c = o_cmp[g * HEAD_DIM:(g + 1) * HEAD_DIM, ci * tq:(ci + 1) * tq]
            y = 0.0
            for br, o in enumerate((o_c, o_sel[ci], o_win[ci])):
                r0 = br * BRANCH_WIDTH + j * LANES + g * HEAD_DIM
                y = y + gate_t[r0:r0 + HEAD_DIM] * o
            parts.append(y)
    o_ref[0] = jnp.concatenate(parts, axis=0).T


def _nsa_prompt(qn, qr, gate, kc, sel, win):
    b, t, _ = qn.shape
    tq, tk = min(ATT_TQ, t), min(NSA_TK, t)
    assert tq == LANES and tk // SEL_BLOCK <= SUBLANES
    assert tk % SEL_BLOCK == 0
    ns = t // SEL_BLOCK
    n_sel = min(N_SELECT, ns)
    n_tiles = t // tk
    qspec = lambda w: pl.BlockSpec((1, tq, w), lambda bi, i: (bi, i, 0))
    seq = lambda n, w: pl.BlockSpec((1, n, w), lambda bi, i: (bi, 0, 0))
    tiles = lambda r, c: pltpu.VMEM((n_tiles, r, c), BF16)
    return pl.pallas_call(
        functools.partial(_nsa_prompt_kernel, tq=tq, tk=tk, n_sel=n_sel, ns=ns),
        grid=(b, t // tq),
        in_specs=[qspec(256), qspec(256), qspec(768), seq(kc.shape[1], 256), seq(t, 256), seq(t, 256)],
        out_specs=qspec(256),
        out_shape=jax.ShapeDtypeStruct((b, t, BRANCH_WIDTH), F32),
        scratch_shapes=[tiles(tk, NSA_KV_WIDTH), tiles(NSA_KV_WIDTH, tk), tiles(tk, NSA_KV_WIDTH),
                        tiles(NSA_KV_WIDTH, tk),
                        pltpu.VMEM((LANES, NSA_KV_WIDTH), BF16), pltpu.VMEM((NSA_KV_WIDTH, LANES), BF16),
                        pltpu.VMEM((2, LANES, tq), F32), pltpu.VMEM((n_tiles, SUBLANES, 2 * tq), F32)],
        compiler_params=_params("parallel", "arbitrary"),
        name="nsa_prompt",
    )(qn, qr, gate, kc, sel, win)


def _compressed_attend(qn, kc, tvec):
    nc = kc.shape[0]
    pad = jnp.zeros((LANES - nc, NSA_KV_WIDTH), F32)
    kck = jnp.concatenate([kc[:, :NSA_KV_WIDTH], pad], axis=0).astype(BF16)
    kcv = jnp.concatenate([kc[:, NSA_KV_WIDTH:], pad], axis=0).astype(BF16)
    lane = lax.broadcasted_iota(jnp.int32, (1, LANES), 1)
    valid = (((lane + 1) * CMP_BLOCK - 1) <= tvec) & (lane < nc)
    s = jnp.where(valid, _dot_nt(qn, kck), NEG)
    e = jnp.exp(s - jnp.max(s, axis=-1, keepdims=True))
    any_valid = jnp.max(valid.astype(F32), axis=-1, keepdims=True)
    p = e / jnp.sum(e, axis=-1, keepdims=True) * any_valid
    return p, _dot(p.astype(BF16), kcv)


def _softmax_scores(raw, masks, vts):
    scores = [jnp.where(msk, s, NEG) for s, msk in zip(raw, masks)]
    m = functools.reduce(jnp.maximum, [jnp.max(s, axis=-1, keepdims=True) for s in scores])
    es = [jnp.exp(s - m) for s in scores]
    l = sum(jnp.sum(e, axis=-1, keepdims=True) for e in es)
    o = sum(_dot_nt(e.astype(BF16), vt) for e, vt in zip(es, vts))
    return o / l


def _nsa_sample_kernel(pt_ref, qn_ref, qr_ref, gate_ref, kc_ref, seln_ref, win_ref, winn_ref, e_ref, *rest,
                       n_pages, nd, past, n_sel, nb):
    pages, (o_ref, wino_ref, newt_ref, impt_ref) = rest[:nb * n_pages], rest[nb * n_pages:]

    @pl.when(pl.program_id(0) == 0)
    def _():
        newt_ref[...] = jnp.zeros_like(newt_ref)

    rows = 4 * nd
    row = lax.broadcasted_iota(jnp.int32, (rows, 1), 0)
    lane = lax.broadcasted_iota(jnp.int32, (1, LANES), 1)
    group_mask = (_div(lane, HEAD_DIM) == _div(row, 2 * nd)).astype(F32)
    tau = _mod(row, nd)
    page = pages[0].shape[1]
    wb = win_ref.shape[2]
    assert page == LANES

    k_half = lambda x: x[:NSA_KV_WIDTH].astype(BF16)
    v_half = lambda x: x[NSA_KV_WIDTH:].astype(BF16)

    ps, o_cmps, o_wins, sel_raw, sel_vts = [], [], [], [], []
    for s in range(nb):
        newt_ref[s, 0, 0:nd, :] = seln_ref[s]
        newt_ref[s, 1, 0:nd, :] = winn_ref[s]
        qn = (qn_ref[s] * SCALE * group_mask).astype(BF16)
        qr = (qr_ref[s] * SCALE * group_mask).astype(BF16)
        p, o_cmp = _compressed_attend(qn, kc_ref[s], past + tau)
        ps.append(p)
        o_cmps.append(o_cmp)

        new_win_t = newt_ref[s, 1].T
        win_tiles = [win_ref[s, :, wi * page:(wi + 1) * page] for wi in range(wb // page)] + [new_win_t]
        masks = [(wi * page + lane > tau + (wb - WINDOW)) & (wi * page + lane >= wb - past)
                 for wi in range(wb // page)] + [lane <= tau]
        o_wins.append(_softmax_scores([_dot(qr, k_half(wt)) for wt in win_tiles], masks,
                                      [v_half(wt) for wt in win_tiles]))
        rolled = pltpu.roll(win_ref[s], wb - nd, 1)
        wino_ref[s, :, 0:wb - page] = rolled[:, 0:wb - page]
        wino_ref[s, :, wb - page:wb] = jnp.where(lane >= page - nd, pltpu.roll(new_win_t, page - nd, 1),
                                                 rolled[:, wb - page:wb])

        sel_tiles = [pages[s * n_pages + pi] for pi in range(n_pages)] + [newt_ref[s, 0].T]
        sel_raw.append([_dot(qr, k_half(st)) for st in sel_tiles])
        sel_vts.append(sel_tiles)

    p_all = jnp.concatenate(ps, axis=0)
    imp = p_all + pltpu.roll(p_all, nd, 0)
    impt_ref[...] = jnp.concatenate([imp, jnp.zeros((LANES - nb * rows, LANES), F32)], axis=0).T
    nbk = -(-((past + nd - 1) // SEL_BLOCK + 1) // SUBLANES) * SUBLANES
    pair = impt_ref[pl.ds(0, nbk, stride=2), :] + impt_ref[pl.ds(1, nbk, stride=2), :]
    sel_t = _rank_select(_block_scores(pair, past + _mod(lane, nd)), n_sel)
    sel_rn = jnp.concatenate([sel_t, jnp.zeros((LANES - nbk, LANES), F32)], axis=0).T[:nb * rows]
    row_all = lax.broadcasted_iota(jnp.int32, (nb * rows, 1), 0)
    sel4_all = jnp.where(_mod(_div(row_all, nd), 2) == 1, sel_rn,
                         pltpu.roll(sel_rn, nb * rows - nd, 0)).astype(BF16)

    for s in range(nb):
        sel4 = sel4_all[s * rows:(s + 1) * rows]
        masks = [_dot(sel4, e_ref[pi]) > 0.5 for pi in range(n_pages)]
        masks.append((_dot(sel4, e_ref[n_pages]) > 0.5) & (lane <= tau))
        o_sel = _softmax_scores(sel_raw[s], masks, [v_half(st) for st in sel_vts[s]])
        y = (gate_ref[s, 0] * o_cmps[s] + gate_ref[s, 1] * o_sel + gate_ref[s, 2] * o_wins[s]) * group_mask
        o_ref[s] = y[0:2 * nd] + y[2 * nd:4 * nd]


def _expand_matrix(n_tiles, tk):
    n = np.arange(LANES)[None, :, None]
    key = (np.arange(n_tiles)[:, None, None] * tk + np.arange(tk)[None, None, :])
    return jnp.asarray(n == key // SEL_BLOCK, BF16)


def _nsa_sample(page_table, qn16, qr16, gate16, kc, sel_new, win_state, win_new, cache_sel, layer, past):
    ns, n_pages = page_table.shape
    nd = sel_new.shape[1]
    assert nd == 4
    width, page = cache_sel.shape[2:]
    wb = win_state.shape[2]
    assert wb % page == 0 and page == LANES
    n_sel = min(N_SELECT, -(-(past + nd) // SEL_BLOCK))
    e = _expand_matrix(n_pages + 1, page)
    nb = math.gcd(ns, SAMPLE_SEQS_PER_STEP)
    per_seq = lambda *s: pl.BlockSpec((nb,) + s, lambda b, pt, n=len(s): (b,) + (0,) * n)
    grid_spec = pltpu.PrefetchScalarGridSpec(
        num_scalar_prefetch=1,
        grid=(ns // nb,),
        in_specs=[per_seq(16, LANES), per_seq(16, LANES), per_seq(3, 16, LANES), per_seq(kc.shape[1], 256),
                  per_seq(nd, 256), per_seq(256, wb), per_seq(nd, 256),
                  pl.BlockSpec(e.shape, lambda b, pt: (0, 0, 0))] + _page_specs(cache_sel, layer, n_pages, nb),
        out_specs=[per_seq(2 * nd, LANES), per_seq(256, wb)],
        scratch_shapes=[pltpu.VMEM((nb, 2, page, 256), F32), pltpu.VMEM((LANES, LANES), F32)],
    )
    return pl.pallas_call(
        functools.partial(_nsa_sample_kernel, n_pages=n_pages, nd=nd, past=past, n_sel=n_sel, nb=nb),
        grid_spec=grid_spec,
        out_shape=[jax.ShapeDtypeStruct((ns, 2 * nd, LANES), F32), jax.ShapeDtypeStruct((ns, 256, wb), F32)],
        compiler_params=_params("arbitrary"),
        name="nsa_sample",
    )(page_table, qn16, qr16, gate16, kc, sel_new, win_state, win_new, e, *([cache_sel] * (nb * n_pages)))


_SC_HALO, _CC_HALO = 8, 32


def _conv_kernel(u_ref, scb_ref, a_ref, uprev_ref, aprev_ref, stsc_ref, stcc_ref,
                 scw_ref, ccw_ref, ccb_ref, ccg_ref, ccbeta_ref,
                 yc_ref, yd_ref, nsc_ref, ncc_ref, fu_ref, fa_ref, *, tc):
    c = pl.program_id(1)

    @pl.when(c == 0)
    def _():
        fu_ref[:, _SC_HALO - (SC_CONV_LEN - 1):_SC_HALO, :] = stsc_ref[...]
        fa_ref[:, _CC_HALO - (CC_CONV_LEN - 1):_CC_HALO, :] = stcc_ref[...]

    @pl.when(c > 0)
    def _():
        fu_ref[:, 0:_SC_HALO, :] = uprev_ref[...]
        fa_ref[:, 0:_CC_HALO, :] = aprev_ref[...]

    fu_ref[:, _SC_HALO:, :] = u_ref[...]
    fa_ref[:, _CC_HALO:, :] = a_ref[...]
    sub = min(CONV_SUB, tc)
    for s in range(tc // sub):
        r0 = s * sub
        acc = 0.0
        for i in range(SC_CONV_LEN):
            o = r0 + i + _SC_HALO - (SC_CONV_LEN - 1)
            acc = acc + scw_ref[i:i + 1, :] * fu_ref[:, o:o + sub, :]
        yc_ref[:, r0:r0 + sub, :] = scb_ref[:, r0:r0 + sub, :] * acc
        acc = 0.0
        for i in range(CC_CONV_LEN):
            o = r0 + i + _CC_HALO - (CC_CONV_LEN - 1)
            acc = acc + ccw_ref[i:i + 1, :] * fa_ref[:, o:o + sub, :]
        z = acc + ccb_ref[...]
        mu = jnp.mean(z, axis=-1, keepdims=True)
        var = jnp.mean(jnp.square(z - mu), axis=-1, keepdims=True)
        ln = (z - mu) * lax.rsqrt(var + NORM_EPS) * ccg_ref[...] + ccbeta_ref[...]
        yd_ref[:, r0:r0 + sub, :] = ln * jax.nn.sigmoid(ln)
    nsc_ref[...] = fu_ref[:, tc + _SC_HALO - (SC_CONV_LEN - 1):tc + _SC_HALO, :]
    ncc_ref[...] = fa_ref[:, tc + _CC_HALO - (CC_CONV_LEN - 1):tc + _CC_HALO, :]


def _conv(u, scb, a, st_sc, st_cc, scw, ccw, ccb, ccg, ccbeta, nb):
    b, t, w = u.shape
    tc = min(CONV_TC, t)
    chunked = t > tc
    if chunked:
        uprev, aprev = u, a
        prev_u = pl.BlockSpec((nb, _SC_HALO, w), lambda bi, c: (bi, jnp.maximum(c * (tc // _SC_HALO) - 1, 0), 0))
        prev_a = pl.BlockSpec((nb, _CC_HALO, w), lambda bi, c: (bi, jnp.maximum(c * (tc // _CC_HALO) - 1, 0), 0))
    else:
        uprev, aprev = jnp.zeros((b, _SC_HALO, w), F32), jnp.zeros((b, _CC_HALO, w), F32)
        prev_u = pl.BlockSpec((nb, _SC_HALO, w), lambda bi, c: (bi, 0, 0))
        prev_a = pl.BlockSpec((nb, _CC_HALO, w), lambda bi, c: (bi, 0, 0))
    cur = pl.BlockSpec((nb, tc, w), lambda bi, c: (bi, c, 0))
    state = lambda n: pl.BlockSpec((nb, n, w), lambda bi, c: (bi, 0, 0))
    wspec = lambda arr: pl.BlockSpec(arr.shape, lambda bi, c: (0, 0))
    return pl.pallas_call(
        functools.partial(_conv_kernel, tc=tc),
        grid=(b // nb, t // tc),
        in_specs=[cur, cur, cur, prev_u, prev_a, state(SC_CONV_LEN - 1), state(CC_CONV_LEN - 1),
                  wspec(scw), wspec(ccw), wspec(ccb), wspec(ccg), wspec(ccbeta)],
        out_specs=[cur, cur, state(SC_CONV_LEN - 1), state(CC_CONV_LEN - 1)],
        out_shape=[jax.ShapeDtypeStruct((b, t, w), F32), jax.ShapeDtypeStruct((b, t, w), F32),
                   jax.ShapeDtypeStruct((b, SC_CONV_LEN - 1, w), F32),
                   jax.ShapeDtypeStruct((b, CC_CONV_LEN - 1, w), F32)],
        scratch_shapes=[pltpu.VMEM((nb, _SC_HALO + tc, w), F32), pltpu.VMEM((nb, _CC_HALO + tc, w), F32)],
        compiler_params=_params("parallel", "arbitrary"),
        name="conv",
    )(u, scb, a, uprev, aprev, st_sc, st_cc, scw, ccw, ccb, ccg, ccbeta)


def _merge_kernel(x_ref, ya_ref, yb_ref, yc_ref, yd_ref, g_ref, wg_ref, bg_ref, wb_ref, wo_ref, o_ref):
    x = x_ref[...]
    d = x.shape[-1]
    h = _rms(x, g_ref[...]).astype(BF16)
    merged = 0.0
    for n, y_ref in enumerate((ya_ref, yb_ref, yc_ref, yd_ref)):
        gate = jax.nn.sigmoid(_dot(h, wg_ref[:, n * d:(n + 1) * d]) + bg_ref[:, n * d:(n + 1) * d])
        merged = merged + gate * _dot(y_ref[...].astype(BF16), wb_ref[n])
    o_ref[...] = x + _dot(merged.astype(BF16), wo_ref[...])


def _merge(x, ya, yb, yc, yd, g, wg, bg, wb, wo):
    n, d = x.shape
    tm = min(TOKEN_TILE, n)
    tok = lambda w: pl.BlockSpec((tm, w), lambda i: (i, 0))
    return pl.pallas_call(
        _merge_kernel,
        grid=(n // tm,),
        in_specs=[tok(d)] + [tok(BRANCH_WIDTH)] * 4 + [_full(a.shape) for a in (g, wg, bg, wb, wo)],
        out_specs=tok(d),
        out_shape=jax.ShapeDtypeStruct((n, d), F32),
        compiler_params=_params("parallel"),
        name="merge",
    )(x, ya, yb, yc, yd, g, wg, bg, wb, wo)


def _w_in_columns():
    splits = (BRANCH_WIDTH,) * 4 + (NSA_KV_WIDTH,) * 6 + (12,) + (BRANCH_WIDTH,) * 5
    offs = np.concatenate([[0], np.cumsum(splits)])
    (o_sbq, o_sbk, _, o_nq, o_ck, _, o_sk, _, o_wk, _, o_ng, o_scb, o_scc, o_sch, o_cca, o_ccg) = offs[:-1]
    r = lambda o, n: np.arange(o, o + n)
    jg_heads = np.concatenate([r((g * 2 + j) * HEAD_DIM, HEAD_DIM) for j in (0, 1) for g in (0, 1)])
    gates = np.concatenate([np.full(HEAD_DIM, o_ng + g * 6 + j * 3 + br)
                            for br in range(3) for j in (0, 1) for g in (0, 1)])
    cols = np.concatenate([r(o_sbq, 256), r(o_sbk, 512), o_nq + jg_heads, r(o_ck, 256), r(o_sk, 256),
                           r(o_wk, 256), gates, r(o_scb, 256), r(o_scc, 256), r(o_sch, 256), r(o_cca, 256),
                           r(o_ccg, 256)])
    assert cols.shape[0] == _P_END
    return cols, jg_heads


def _rope_tables(pos):
    half = HEAD_DIM // 2
    inv_freq = jnp.exp(-math.log(ROPE_THETA) * jnp.arange(half, dtype=F32) / half)
    ang = pos.astype(F32)[:, None] * inv_freq[None, :]
    cos, sin = jnp.cos(ang), jnp.sin(ang)
    reps = BRANCH_WIDTH // HEAD_DIM
    return jnp.tile(jnp.concatenate([cos, cos], -1), (1, reps)), jnp.tile(jnp.concatenate([-sin, sin], -1), (1, reps))


def _rows16(a, nd):
    ns = a.shape[0]
    a = a.reshape(ns, nd, 2, LANES).transpose(0, 2, 1, 3)
    return jnp.broadcast_to(a[:, None], (ns, 2, 2, nd, LANES)).reshape(ns, 4 * nd, LANES)


def _feature_major(cache):
    l, p, rows = cache.shape[:3]
    return cache.transpose(0, 1, 3, 4, 5, 2).reshape(l, p, -1, rows)


def kernel(x_prompt, x_sample, cache_sb_kv, cache_cmp_kv, cache_sel_kv, state_win_kv, state_conv_sc,
           state_conv_cc, page_table, norm_ffn1, w_ffn1_gu, w_ffn1_down, norm_mix, w_in, nsa_q_gain,
           nsa_k_gain, nsa_cmp_pe, nsa_cmp_w, sc_conv_w, cc_conv_w, cc_conv_b, cc_norm_g, cc_norm_b,
           w_branch, w_gate, b_gate, w_out, norm_ffn2, w_ffn2_gu, w_ffn2_down):
    b, t, d = x_prompt.shape
    ns, nd, _ = x_sample.shape
    depth, n_phys, page = cache_sb_kv.shape[:3]
    n_pages = page_table.shape[1]
    past = n_pages * page
    ffn_dim = w_ffn1_down.shape[1]
    n_chunk = ffn_dim // FFN_CHUNK
    wb = state_win_kv.shape[2]

    def gu(w):
        return w.reshape(depth, d, 2, n_chunk, FFN_CHUNK).transpose(0, 2, 3, 1, 4).astype(BF16)

    wgu1, wgu2 = gu(w_ffn1_gu), gu(w_ffn2_gu)
    wd1 = w_ffn1_down.reshape(depth, n_chunk, FFN_CHUNK, d).astype(BF16)
    wd2 = w_ffn2_down.reshape(depth, n_chunk, FFN_CHUNK, d).astype(BF16)
    cols, jg_heads = _w_in_columns()
    w_in_r = w_in[:, :, cols].astype(BF16)
    w_gate_b = w_gate.astype(BF16)
    w_branch_b = w_branch.at[:, 1].set(w_branch[:, 1][:, jg_heads, :]).astype(BF16)
    w_out_b = w_out.astype(BF16)
    eye2 = jnp.eye(2, dtype=F32)
    w_cmp = jnp.einsum("Lklde,gG->LklgdGe", nsa_cmp_w, eye2).reshape(depth, 2, CMP_BLOCK, 128, 128)
    w_cmp = w_cmp.astype(BF16)
    pe_rows = jnp.tile(nsa_cmp_pe, (1, 1, 1, 2))
    row2 = lambda a: a.reshape(depth, 1, -1)
    n1, nm, n2, bg = row2(norm_ffn1), row2(norm_mix), row2(norm_ffn2), row2(b_gate)
    qg = jnp.tile(row2(nsa_q_gain), (1, 1, 4))
    kg = jnp.tile(row2(nsa_k_gain), (1, 1, 2))
    ccb, ccg, ccbeta = row2(cc_conv_b), row2(cc_norm_g), row2(cc_norm_b)
    hd = np.arange(BRANCH_WIDTH) // HEAD_DIM
    bd = jnp.asarray(hd[:, None] == hd[None, :], BF16)
    bd1 = bd[:LANES, :LANES]
    cos_p, sin_p = _rope_tables(jnp.arange(t, dtype=jnp.int32))
    cos_s, sin_s = _rope_tables(past + jnp.arange(ns * nd, dtype=jnp.int32) % nd)

    cache_sb = _feature_major(cache_sb_kv)
    cache_cmp = _feature_major(cache_cmp_kv)
    cache_sel = _feature_major(cache_sel_kv)
    win_state = _feature_major(state_win_kv)

    xp = x_prompt.reshape(b * t, d)
    xs = x_sample.reshape(ns * nd, d)
    zeros_sc = jnp.zeros((b, SC_CONV_LEN - 1, BRANCH_WIDTH), F32)
    zeros_cc = jnp.zeros((b, CC_CONV_LEN - 1, BRANCH_WIDTH), F32)
    outs_p = [[] for _ in range(6)]
    outs_s = [[] for _ in range(6)]
    for l in range(depth):
        xp = _ffn(xp, n1[l], wgu1[l], wd1[l])
        xs = _ffn(xs, n1[l], wgu1[l], wd1[l])

        (sbq, sbkv, qn, qr, cmp_, sel, win, gate, scb, scu, cca) = _proj(
            xp, nm[l], w_in_r[l], cos_p, sin_p, qg[l], kg[l], bd)
        seq = lambda a: a.reshape(b, t, a.shape[-1])
        ya = _sb_prompt(seq(sbq), seq(sbkv))
        kc = _compress_prompt(seq(cmp_), pe_rows[l], w_cmp[l], kg[l], bd1)
        yb = _nsa_prompt(seq(qn), seq(qr), seq(gate), kc, seq(sel), seq(win))
        yc, yd, nsc, ncc = _conv(seq(scu), seq(scb), seq(cca), zeros_sc, zeros_cc, sc_conv_w[l], cc_conv_w[l],
                                 ccb[l], ccg[l], ccbeta[l], nb=1)
        flat = lambda a: a.reshape(b * t, a.shape[-1])
        xp = _merge(xp, flat(ya), flat(yb), flat(yc), flat(yd), nm[l], w_gate_b[l], bg[l], w_branch_b[l],
                    w_out_b[l])
        for lst, a in zip(outs_p, (seq(sbkv), seq(cmp_), seq(sel), seq(win)[:, t - min(WINDOW, t):], nsc, ncc)):
            lst.append(a)

        (sbq, sbkv, qn, qr, cmp_, sel, win, gate, scb, scu, cca) = _proj(
            xs, nm[l], w_in_r[l], cos_s, sin_s, qg[l], kg[l], bd)
        seq = lambda a: a.reshape(ns, nd, a.shape[-1])
        q16 = jnp.tile(seq(sbq), (1, 4, 1))
        ya = _sb_sample(page_table, q16, seq(sbkv), cache_sb, l)
        kc = _compress_sample(page_table, cache_cmp, l, pe_rows[l], w_cmp[l], kg[l], bd1)
        gate16 = jnp.stack([_rows16(seq(gate)[:, :, br * 256:(br + 1) * 256], nd) for br in range(3)], axis=1)
        y8, win_out = _nsa_sample(page_table, _rows16(seq(qn), nd), _rows16(seq(qr), nd), gate16, kc,
                                  seq(sel), win_state[l], seq(win), cache_sel, l, past)
        yb = y8.reshape(ns, 2, nd, LANES).transpose(0, 2, 1, 3).reshape(ns, nd, BRANCH_WIDTH)
        yc, yd, nsc, ncc = _conv(seq(scu), seq(scb), seq(cca), state_conv_sc[l], state_conv_cc[l],
                                 sc_conv_w[l], cc_conv_w[l], ccb[l], ccg[l], ccbeta[l], nb=min(8, ns))
        flat = lambda a: a.reshape(ns * nd, a.shape[-1])
        xs = _merge(xs, flat(ya), flat(yb), flat(yc), flat(yd), nm[l], w_gate_b[l], bg[l], w_branch_b[l],
                    w_out_b[l])
        for lst, a in zip(outs_s, (seq(sbkv), seq(cmp_), seq(sel), win_out, nsc, ncc)):
            lst.append(a)

        xp = _ffn(xp, n2[l], wgu2[l], wd2[l])
        xs = _ffn(xs, n2[l], wgu2[l], wd2[l])

    def kv5(a, heads):
        a = jnp.stack(a)
        return a.reshape(a.shape[:3] + (2, heads, HEAD_DIM))

    sb_heads, nsa_kv_heads = BRANCH_WIDTH // HEAD_DIM, NSA_KV_WIDTH // HEAD_DIM
    win_s = jnp.stack(outs_s[3])
    win_s = win_s.reshape(depth, ns, 2, nsa_kv_heads, HEAD_DIM, wb).transpose(0, 1, 5, 2, 3, 4)
    return (xp.reshape(b, t, d), xs.reshape(ns, nd, d),
            kv5(outs_p[0], sb_heads), kv5(outs_s[0], sb_heads),
            kv5(outs_p[1], nsa_kv_heads), kv5(outs_s[1], nsa_kv_heads),
            kv5(outs_p[2], nsa_kv_heads), kv5(outs_s[2], nsa_kv_heads),
            kv5(outs_p[3], nsa_kv_heads), win_s,
            jnp.stack(outs_p[4]), jnp.stack(outs_s[4]), jnp.stack(outs_p[5]), jnp.stack(outs_s[5]))
```

```python
import functools
import math

import numpy as np
import jax
import jax.numpy as jnp
from jax import lax
from jax.experimental import pallas as pl
from jax.experimental.pallas import tpu as pltpu

F32, BF16 = jnp.float32, jnp.bfloat16

HEAD_DIM = 64
BRANCH_WIDTH = 256
N_BRANCH = 4
NSA_KV_WIDTH = 128
CMP_BLOCK = 32
SEL_BLOCK = 64
N_SELECT = 16
WINDOW = 512
SC_CONV_LEN = 3
CC_CONV_LEN = 31
ROPE_THETA = 10000.0
NORM_EPS = 1e-6
NEG = -1e30
SCALE = HEAD_DIM ** -0.5

LANES = 128
SUBLANES = 8
VMEM_LIMIT_BYTES = 56 * 1024 * 1024
TOKEN_TILE = 512
FFN_CHUNK = 256
ATT_TQ = 128
SB_TK = 256
NSA_TK = 256
SAMPLE_SEQS_PER_STEP = 4
CMP_PITCH = CMP_BLOCK + 1
CONV_TC = 256
CONV_SUB = 64

_NT = (((1,), (1,)), ((), ()))


def _params(*sem):
    return pltpu.CompilerParams(dimension_semantics=sem, vmem_limit_bytes=VMEM_LIMIT_BYTES)


def _full(shape):
    n = len(shape)
    return pl.BlockSpec(shape, lambda *_, n=n: (0,) * n)


def _dot(a, b):
    return jnp.dot(a, b, preferred_element_type=F32)


def _dot_nt(a, b):
    return lax.dot_general(a, b, _NT, preferred_element_type=F32)


def _div(x, n):
    assert n & (n - 1) == 0
    return x >> (n.bit_length() - 1)


def _mod(x, n):
    assert n & (n - 1) == 0
    return x & (n - 1)


def _rms(x, g):
    return x * lax.rsqrt(jnp.mean(x * x, axis=-1, keepdims=True) + NORM_EPS) * g


def _split(x):
    hi = x.astype(BF16)
    return hi, (x - hi.astype(F32)).astype(BF16)


def _split_dot(x, m):
    hi, lo = _split(x)
    return _dot(hi, m) + _dot(lo, m)


def _head_rms(x, bd, g):
    ms = _split_dot(x * x, bd) * (1.0 / HEAD_DIM)
    return x * lax.rsqrt(ms + NORM_EPS) * g


def _rope(x, cos, sin_signed):
    w = x.shape[-1]
    lane = lax.broadcasted_iota(jnp.int32, x.shape, 1)
    first_half = (lane & (HEAD_DIM - 1)) < HEAD_DIM // 2
    rot = jnp.where(first_half, pltpu.roll(x, w - HEAD_DIM // 2, 1), pltpu.roll(x, HEAD_DIM // 2, 1))
    return x * cos + rot * sin_signed


def _ffn_kernel(x_ref, g_ref, wgu_ref, wd_ref, o_ref, h_ref, acc_ref):
    f = wd_ref.shape[0]
    h_ref[...] = _rms(x_ref[...], g_ref[...]).astype(BF16)
    for c in range(f // FFN_CHUNK):
        cols = slice(c * FFN_CHUNK, (c + 1) * FFN_CHUNK)
        h = h_ref[...]
        g = _dot(h, wgu_ref[:, cols])
        u = _dot(h, wgu_ref[:, f + c * FFN_CHUNK: f + (c + 1) * FFN_CHUNK])
        a = (g * jax.nn.sigmoid(g) * u).astype(BF16)
        down = _dot(a, wd_ref[cols, :])
        if c == 0:
            acc_ref[...] = down
        else:
            acc_ref[...] += down
    o_ref[...] = x_ref[...] + 0.5 * acc_ref[...]


def _ffn(x, g, wgu, wd):
    n, d = x.shape
    tm = min(TOKEN_TILE, n)
    return pl.pallas_call(
        _ffn_kernel,
        grid=(n // tm,),
        in_specs=[pl.BlockSpec((tm, d), lambda i: (i, 0)), _full(g.shape), _full(wgu.shape), _full(wd.shape)],
        out_specs=pl.BlockSpec((tm, d), lambda i: (i, 0)),
        out_shape=jax.ShapeDtypeStruct((n, d), F32),
        scratch_shapes=[pltpu.VMEM((tm, d), BF16), pltpu.VMEM((tm, d), F32)],
        compiler_params=_params("parallel"),
        name="ffn",
    )(x, g, wgu, wd)


_P_SBQ, _P_SBKV, _P_NQ, _P_CMP, _P_SEL, _P_WIN, _P_GATE = 0, 256, 768, 1024, 1280, 1536, 1792
_P_SCB, _P_SCC, _P_SCH, _P_CCA, _P_CCG, _P_END = 2560, 2816, 3072, 3328, 3584, 3840


def _proj_kernel(x_ref, g_ref, w_ref, cos_ref, sin_ref, qg_ref, kg_ref, bd_ref,
                 sbq_ref, sbkv_ref, qn_ref, qr_ref, cmp_ref, sel_ref, win_ref, gate_ref,
                 scb_ref, scu_ref, cca_ref):
    h = _rms(x_ref[...], g_ref[...]).astype(BF16)

    def mm(a, b):
        return _dot(h, w_ref[:, a:b])

    cos, sin, bd = cos_ref[...], sin_ref[...], bd_ref[...]
    sbq_ref[...] = mm(_P_SBQ, _P_SBKV)
    sbkv_ref[...] = mm(_P_SBKV, _P_NQ)
    qn = _head_rms(mm(_P_NQ, _P_CMP), bd, qg_ref[...])
    qn_ref[...] = qn
    qr_ref[...] = _rope(qn, cos, sin)
    cmp_ref[...] = mm(_P_CMP, _P_SEL)
    bd1, cos1, sin1 = bd[:LANES, :LANES], cos[:, :LANES], sin[:, :LANES]
    for a, ref in ((_P_SEL, sel_ref), (_P_WIN, win_ref)):
        kv = mm(a, a + 2 * NSA_KV_WIDTH)
        ref[:, :NSA_KV_WIDTH] = _rope(_head_rms(kv[:, :NSA_KV_WIDTH], bd1, kg_ref[...]), cos1, sin1)
        ref[:, NSA_KV_WIDTH:] = kv[:, NSA_KV_WIDTH:]
    gate_ref[...] = jax.nn.sigmoid(mm(_P_GATE, _P_SCB))
    scb_ref[...] = mm(_P_SCB, _P_SCC)
    scu_ref[...] = mm(_P_SCC, _P_SCH) * mm(_P_SCH, _P_CCA)
    cca_ref[...] = mm(_P_CCA, _P_CCG) * jax.nn.sigmoid(mm(_P_CCG, _P_END))


def _proj(x, g, w, cos, sin, qg, kg, bd):
    n, d = x.shape
    tm = min(TOKEN_TILE, n)
    n_rope_tiles = cos.shape[0] // tm
    widths = (256, 512, 256, 256, 256, 256, 256, 768, 256, 256, 256)
    tok = lambda wd_: pl.BlockSpec((tm, wd_), lambda i: (i, 0))
    rope = pl.BlockSpec((tm, cos.shape[1]), lambda i: (i % n_rope_tiles, 0))
    return pl.pallas_call(
        _proj_kernel,
        grid=(n // tm,),
        in_specs=[tok(d), _full(g.shape), _full(w.shape), rope, rope, _full(qg.shape), _full(kg.shape),
                  _full(bd.shape)],
        out_specs=[tok(wd_) for wd_ in widths],
        out_shape=[jax.ShapeDtypeStruct((n, wd_), F32) for wd_ in widths],
        compiler_params=_params("parallel"),
        name="proj",
    )(x, g, w, cos, sin, qg, kg, bd)


def _sb_logs(z):
    log_beta = jnp.minimum(z, 0.0) - jnp.log(1.0 + jnp.exp(-jnp.abs(z)))
    return log_beta, log_beta - z


def _sb_tile_t(k, vts, q_ts, valid, carry, tri_t):
    logs = [_sb_logs(_dot(k, q)) for q in q_ts]
    keeps = [lk if valid is None else jnp.where(valid, lk, 0.0) for _, lk in logs]
    between = [_dot(tri_t, hi) + _dot(tri_t, lo) for hi, lo in [_split(lk) for lk in keeps]]
    ws = [jnp.exp(lb + bt + c) for (lb, _), bt, (c, _) in zip(logs, between, carry)]
    if valid is not None:
        ws = [jnp.where(valid, w, 0.0) for w in ws]
    pvs = [_dot(vt, w.astype(BF16)) for vt, w in zip(vts, ws)]
    return tuple((c + bt[0:1] + lk[0:1], acc + pv)
                 for (c, acc), bt, lk, pv in zip(carry, between, keeps, pvs))


def _sb_prompt_kernel(q_ref, kv_ref, tri_ref, o_ref, kb_ref, vtb_ref, *, tq, tk):
    i = pl.program_id(1)
    n_all = kb_ref.shape[0]

    @pl.when(i == 0)
    def _():
        for j in range(n_all):
            tile = kv_ref[0, j * tk:(j + 1) * tk, :]
            kb_ref[j] = tile[:, :BRANCH_WIDTH].astype(BF16)
            vtb_ref[j] = tile[:, BRANCH_WIDTH:].T.astype(BF16)

    t0 = i * tq
    heads = BRANCH_WIDTH // HEAD_DIM
    feat = lax.broadcasted_iota(jnp.int32, (BRANCH_WIDTH, 1), 0)
    q_feat = q_ref[0].T * SCALE
    q_h = [jnp.where(_div(feat, HEAD_DIM) == h, q_feat, 0.0).astype(BF16) for h in range(heads)]
    tvec = t0 + lax.broadcasted_iota(jnp.int32, (1, tq), 1)
    tri_t = tri_ref[...]
    j_diag = (t0 + tq - 1) // tk
    kpos = j_diag * tk + lax.broadcasted_iota(jnp.int32, (tk, 1), 0)

    def tile(j, carry, valid):
        vts = [vtb_ref[j, h * HEAD_DIM:(h + 1) * HEAD_DIM, :] for h in range(heads)]
        return _sb_tile_t(kb_ref[j], vts, q_h, valid, carry, tri_t)

    init = tuple((jnp.zeros((1, tq), F32), jnp.zeros((HEAD_DIM, tq), F32)) for _ in range(heads))
    carry = tile(j_diag, init, kpos < tvec)
    carry = lax.fori_loop(0, j_diag, lambda it, carry: tile(j_diag - 1 - it, carry, None), carry)
    o_ref[0] = jnp.concatenate([acc for _, acc in carry], axis=0).T


def _tri(n, upper):
    r = np.arange(n)
    m = r[:, None] < r[None, :] if upper else r[:, None] > r[None, :]
    return jnp.asarray(m, BF16)


def _sb_prompt(q, kv):
    b, t, _ = q.shape
    tq, tk = min(ATT_TQ, t), min(SB_TK, t)
    assert tk % tq == 0
    return pl.pallas_call(
        functools.partial(_sb_prompt_kernel, tq=tq, tk=tk),
        grid=(b, t // tq),
        in_specs=[pl.BlockSpec((1, tq, BRANCH_WIDTH), lambda bi, i: (bi, i, 0)),
                  pl.BlockSpec((1, t, 2 * BRANCH_WIDTH), lambda bi, i: (bi, 0, 0)),
                  _full((tk, tk))],
        out_specs=pl.BlockSpec((1, tq, BRANCH_WIDTH), lambda bi, i: (bi, i, 0)),
        out_shape=jax.ShapeDtypeStruct((b, t, BRANCH_WIDTH), F32),
        scratch_shapes=[pltpu.VMEM((t // tk, tk, BRANCH_WIDTH), BF16),
                        pltpu.VMEM((t // tk, BRANCH_WIDTH, tk), BF16)],
        compiler_params=_params("parallel", "arbitrary"),
        name="sb_prompt",
    )(q, kv, _tri(tk, upper=True))


def _sb_sample_kernel(pt_ref, q_ref, new_ref, tri_ref, *rest, n_pages, nd, nb):
    pages, (o_ref, newt_ref) = rest[:nb * n_pages], rest[nb * n_pages:]

    @pl.when(pl.program_id(0) == 0)
    def _():
        newt_ref[...] = jnp.zeros_like(newt_ref)

    rows = 4 * nd
    row = lax.broadcasted_iota(jnp.int32, (rows, 1), 0)
    lane = lax.broadcasted_iota(jnp.int32, (1, BRANCH_WIDTH), 1)
    head_mask = (_div(lane, HEAD_DIM) == _div(row, nd)).astype(F32)
    tri = tri_ref[...]
    key = lax.broadcasted_iota(jnp.int32, (1, newt_ref.shape[1]), 1)
    kv_half = lambda x, v: x[v * BRANCH_WIDTH:(v + 1) * BRANCH_WIDTH].astype(BF16)
    qms, tiles = [], []
    for s in range(nb):
        newt_ref[s, 0:nd, :] = new_ref[s]
        qms.append((q_ref[s] * SCALE * head_mask).astype(BF16))
        new_t = newt_ref[s].T
        seq_tiles = [(functools.partial(kv_half, new_t), key < _mod(row, nd))]
        for p in reversed(range(n_pages)):
            page_ref = pages[s * n_pages + p]
            seq_tiles.append((lambda v, r=page_ref: kv_half(r, v), None))
        tiles.append(seq_tiles)
    logs = [[_sb_logs(_dot(qm, get(0))) for get, _ in seq] for qm, seq in zip(qms, tiles)]
    keeps = [[lk if valid is None else jnp.where(valid, lk, 0.0) for (_, lk), (_, valid) in zip(lg, seq)]
             for lg, seq in zip(logs, tiles)]
    between = [[_split_dot(lk, tri) for lk in ks] for ks in keeps]
    weights = []
    for lg, ks, bt, seq in zip(logs, keeps, between, tiles):
        c = jnp.zeros((rows, 1), F32)
        ws = []
        for (lb, _), lk, b, (_, valid) in zip(lg, ks, bt, seq):
            w = jnp.exp(lb + b + c)
            ws.append((w if valid is None else jnp.where(valid, w, 0.0)).astype(BF16))
            c = c + jnp.sum(lk, axis=-1, keepdims=True)
        weights.append(ws)
    for s in range(nb):
        acc = sum(_dot_nt(w, get(1)) for w, (get, _) in zip(weights[s], tiles[s]))
        m = acc * head_mask
        hs = m[0:8] + m[8:16]
        o_ref[s] = hs + pltpu.roll(hs, 4, 0)


def _page_specs(cache, layer, n_pages, nb):
    width, page = cache.shape[2:]
    return [pl.BlockSpec((None, None, width, page), lambda b, pt, s=s, p=p: (layer, pt[b * nb + s, p], 0, 0))
            for s in range(nb) for p in range(n_pages)]


def _sb_sample(page_table, q16, new, cache, layer):
    ns, n_pages = page_table.shape
    nd = new.shape[1]
    assert nd == 4 and q16.shape[1] == 16
    width, page = cache.shape[2:]
    nb = math.gcd(ns, SAMPLE_SEQS_PER_STEP)
    grid_spec = pltpu.PrefetchScalarGridSpec(
        num_scalar_prefetch=1,
        grid=(ns // nb,),
        in_specs=[pl.BlockSpec((nb, 16, BRANCH_WIDTH), lambda b, pt: (b, 0, 0)),
                  pl.BlockSpec((nb, nd, width), lambda b, pt: (b, 0, 0)),
                  pl.BlockSpec((page, page), lambda b, pt: (0, 0))] + _page_specs(cache, layer, n_pages, nb),
        out_specs=pl.BlockSpec((nb, 8, BRANCH_WIDTH), lambda b, pt: (b, 0, 0)),
        scratch_shapes=[pltpu.VMEM((nb, page, width), F32)],
    )
    out = pl.pallas_call(
        functools.partial(_sb_sample_kernel, n_pages=n_pages, nd=nd, nb=nb),
        grid_spec=grid_spec,
        out_shape=jax.ShapeDtypeStruct((ns, 8, BRANCH_WIDTH), F32),
        compiler_params=_params("arbitrary"),
        name="sb_sample",
    )(page_table, q16, new, _tri(page, upper=False), *([cache] * (nb * n_pages)))
    return out[:, :nd]


def _compress_math(x_ref, nc, pitch, pe_ref, w_ref, kg, bd1):
    outs = []
    for kv in range(2):
        acc = jnp.zeros((nc, NSA_KV_WIDTH), F32)
        for l in range(CMP_BLOCK):
            xl = x_ref[kv, pl.ds(l, nc, stride=pitch), :] + pe_ref[kv, l:l + 1, :]
            acc = acc + _dot(xl.astype(BF16), w_ref[kv, l])
        outs.append(acc)
    return _head_rms(outs[0], bd1, kg), outs[1]


def _compress_prompt_kernel(x_ref, pe_ref, w_ref, kg_ref, bd_ref, o_ref, xs_ref):
    xs_ref[0] = x_ref[0, :, :NSA_KV_WIDTH]
    xs_ref[1] = x_ref[0, :, NSA_KV_WIDTH:]
    k, v = _compress_math(xs_ref, o_ref.shape[1], CMP_BLOCK, pe_ref, w_ref, kg_ref[...], bd_ref[...])
    o_ref[0, :, :NSA_KV_WIDTH] = k
    o_ref[0, :, NSA_KV_WIDTH:] = v


def _compress_prompt(x, pe, w, kg, bd1):
    b, t, _ = x.shape
    nc = t // CMP_BLOCK
    return pl.pallas_call(
        _compress_prompt_kernel,
        grid=(b,),
        in_specs=[pl.BlockSpec((1, t, 2 * NSA_KV_WIDTH), lambda i: (i, 0, 0)), _full(pe.shape), _full(w.shape),
                  _full(kg.shape), _full(bd1.shape)],
        out_specs=pl.BlockSpec((1, nc, 2 * NSA_KV_WIDTH), lambda i: (i, 0, 0)),
        out_shape=jax.ShapeDtypeStruct((b, nc, 2 * NSA_KV_WIDTH), F32),
        scratch_shapes=[pltpu.VMEM((2, t, NSA_KV_WIDTH), F32)],
        compiler_params=_params("parallel"),
        name="compress_prompt",
    )(x, pe, w, kg, bd1)


def _compress_sample_kernel(pt_ref, pe_ref, w_ref, kg_ref, bd_ref, *rest, n_pages, nb):
    pages, (o_ref, x_ref) = rest[:nb * n_pages], rest[nb * n_pages:]
    per_page = pages[0].shape[1] // CMP_BLOCK
    nc = o_ref.shape[1]
    for s in range(nb):
        for p in range(n_pages):
            for kv in range(2):
                rows = pages[s * n_pages + p][kv * NSA_KV_WIDTH:(kv + 1) * NSA_KV_WIDTH, :].T
                for q in range(per_page):
                    n = s * nc + p * per_page + q
                    x_ref[kv, pl.ds(n * CMP_PITCH, CMP_BLOCK), :] = rows[q * CMP_BLOCK:(q + 1) * CMP_BLOCK]
    k, v = _compress_math(x_ref, nb * nc, CMP_PITCH, pe_ref, w_ref, kg_ref[...], bd_ref[...])
    for s in range(nb):
        o_ref[s, :, :NSA_KV_WIDTH] = k[s * nc:(s + 1) * nc]
        o_ref[s, :, NSA_KV_WIDTH:] = v[s * nc:(s + 1) * nc]


def _compress_sample(page_table, cache, layer, pe, w, kg, bd1):
    ns, n_pages = page_table.shape
    width, page = cache.shape[2:]
    nc = n_pages * page // CMP_BLOCK
    nb = math.gcd(ns, SAMPLE_SEQS_PER_STEP)
    const = lambda shape: pl.BlockSpec(shape, lambda b, pt, n=len(shape): (0,) * n)
    grid_spec = pltpu.PrefetchScalarGridSpec(
        num_scalar_prefetch=1,
        grid=(ns // nb,),
        in_specs=[const(pe.shape), const(w.shape), const(kg.shape), const(bd1.shape)]
        + _page_specs(cache, layer, n_pages, nb),
        out_specs=pl.BlockSpec((nb, nc, 2 * NSA_KV_WIDTH), lambda b, pt: (b, 0, 0)),
        scratch_shapes=[pltpu.VMEM((2, -(-nb * nc * CMP_PITCH // SUBLANES) * SUBLANES, NSA_KV_WIDTH), F32)],
    )
    return pl.pallas_call(
        functools.partial(_compress_sample_kernel, n_pages=n_pages, nb=nb),
        grid_spec=grid_spec,
        out_shape=jax.ShapeDtypeStruct((ns, nc, 2 * NSA_KV_WIDTH), F32),
        compiler_params=_params("arbitrary"),
        name="compress_sample",
    )(page_table, pe, w, kg, bd1, *([cache] * (nb * n_pages)))


def _flash_t(q_ts, k_ref, vt_ref, lo, hi, tk, mask_fn):
    tq = q_ts[0].shape[1]

    def body(j, carry):
        k, vt = k_ref[j], vt_ref[j]
        masks = mask_fn(j, j * tk + lax.broadcasted_iota(jnp.int32, (tk, 1), 0))
        ss = [jnp.where(masks[ci // 2], _dot(k, q), NEG) for ci, q in enumerate(q_ts)]
        m2s = [jnp.maximum(m, jnp.max(s, axis=0, keepdims=True)) for s, (m, _, _) in zip(ss, carry)]
        ps = [jnp.exp(s - m2) for s, m2 in zip(ss, m2s)]
        pvs = [_dot(vt[(ci // 2) * HEAD_DIM:(ci // 2 + 1) * HEAD_DIM], p.astype(BF16)) for ci, p in enumerate(ps)]
        out = []
        for (m, l, acc), m2, p, pv in zip(carry, m2s, ps, pvs):
            a = jnp.exp(m - m2)
            out.append((m2, a * l + jnp.sum(p, axis=0, keepdims=True), a * acc + pv))
        return tuple(out)

    init = tuple((jnp.full((1, tq), NEG, F32), jnp.zeros((1, tq), F32), jnp.zeros((HEAD_DIM, tq), F32))
                 for _ in q_ts)
    return [acc / l for _, l, acc in lax.fori_loop(lo, hi, body, init)]


def _rank_select(score, n_sel):
    nb = score.shape[0]
    row = lax.broadcasted_iota(jnp.int32, (nb, 1), 0)
    ahead = jnp.zeros(score.shape, F32)
    for m in range(nb):
        sm = score[m:m + 1, :]
        ahead = ahead + jnp.where((sm > score) | ((sm == score) & (row > m)), 1.0, 0.0)
    return (ahead < n_sel).astype(F32)


def _block_scores(pair, tvec):
    n = lax.broadcasted_iota(jnp.int32, (pair.shape[0], 1), 0)
    cur = _div(tvec, SEL_BLOCK)
    forced = (n == 0) | (n == cur) | (n == cur - 1)
    return jnp.where(forced, 1e30, jnp.where(n * SEL_BLOCK <= tvec, pair, -1.0))


def _nsa_prompt_kernel(qn_ref, qr_ref, gate_ref, kc_ref, sel_ref, win_ref, o_ref,
                       selk_ref, selvt_ref, wink_ref, winvt_ref, kck_ref, kcvt_ref, imp_ref, sel8_ref,
                       *, tq, tk, n_sel, ns):
    i = pl.program_id(1)
    n_all = selk_ref.shape[0]
    nc = kc_ref.shape[1]
    blocks_per_tile = tk // SEL_BLOCK

    @pl.when(i == 0)
    def _():
        for src, k_ref, vt_ref in ((sel_ref, selk_ref, selvt_ref), (win_ref, wink_ref, winvt_ref)):
            for j in range(n_all):
                tile = src[0, j * tk:(j + 1) * tk, :]
                k_ref[j] = tile[:, :NSA_KV_WIDTH].astype(BF16)
                vt_ref[j] = tile[:, NSA_KV_WIDTH:].T.astype(BF16)
        kc = kc_ref[0]
        pad = jnp.zeros((LANES - nc, NSA_KV_WIDTH), F32)
        kck_ref[...] = jnp.concatenate([kc[:, :NSA_KV_WIDTH], pad], axis=0).astype(BF16)
        kcvt_ref[...] = jnp.concatenate([kc[:, NSA_KV_WIDTH:], pad], axis=0).T.astype(BF16)
        sel8_ref[...] = jnp.zeros_like(sel8_ref)

    t0 = i * tq
    feat = lax.broadcasted_iota(jnp.int32, (LANES, 1), 0)
    tlane = lax.broadcasted_iota(jnp.int32, (1, tq), 1)
    tvec = t0 + jnp.concatenate([tlane] * 4, axis=1)
    tvec2 = t0 + jnp.concatenate([tlane] * 2, axis=1)

    def stack_t(q):
        q_feat = q.T * SCALE
        parts = [jnp.where(_div(feat, HEAD_DIM) == g, q_feat[j * LANES:(j + 1) * LANES], 0.0)
                 for g in (0, 1) for j in (0, 1)]
        return jnp.concatenate(parts, axis=1).astype(BF16)

    qn_t, qr_t = stack_t(qn_ref[0]), stack_t(qr_ref[0])

    c = lax.broadcasted_iota(jnp.int32, (LANES, 1), 0)
    cvalid = (((c + 1) * CMP_BLOCK - 1) <= tvec) & (c < nc)
    s = jnp.where(cvalid, _dot(kck_ref[...], qn_t), NEG)
    e = jnp.exp(s - jnp.max(s, axis=0, keepdims=True))
    any_valid = jnp.max(cvalid.astype(F32), axis=0, keepdims=True)
    p = e / jnp.sum(e, axis=0, keepdims=True) * any_valid
    o_cmp = _dot(kcvt_ref[...], p.astype(BF16))

    imp_ref[0] = p[:, 0:tq] + p[:, tq:2 * tq]
    imp_ref[1] = p[:, 2 * tq:3 * tq] + p[:, 3 * tq:4 * tq]
    nb = sel8_ref.shape[0] * blocks_per_tile
    pair = jnp.concatenate([imp_ref[g, pl.ds(0, nb, stride=2), :] + imp_ref[g, pl.ds(1, nb, stride=2), :]
                            for g in (0, 1)], axis=1)
    sel = _rank_select(_block_scores(pair, tvec2), n_sel)
    for j in range(n_all):
        sel8_ref[j, 0:blocks_per_tile, :] = sel[j * blocks_per_tile:(j + 1) * blocks_per_tile, :]

    hi = (t0 + tq - 1) // tk + 1
    krow = lax.broadcasted_iota(jnp.int32, (tk, 1), 0)

    tq_vec = t0 + tlane

    def sel_mask(j, kpos):
        s8 = sel8_ref[j]
        chosen = s8[0:1, :]
        for bi in range(1, blocks_per_tile):
            chosen = jnp.where(krow >= bi * SEL_BLOCK, s8[bi:bi + 1, :], chosen)
        causal = kpos <= tq_vec
        return [(chosen[:, g * tq:(g + 1) * tq] > 0.5) & causal for g in (0, 1)]

    def win_mask(j, kpos):
        inside = (kpos <= tq_vec) & (kpos > tq_vec - WINDOW)
        return [inside, inside]

    qr_ts = [qr_t[:, ci * tq:(ci + 1) * tq] for ci in range(4)]
    o_sel = _flash_t(qr_ts, selk_ref, selvt_ref, 0, hi, tk, sel_mask)
    lo = jnp.maximum(t0 - WINDOW + 1, 0) // tk
    o_win = _flash_t(qr_ts, wink_ref, winvt_ref, lo, hi, tk, win_mask)

    gate_t = gate_ref[0].T
    parts = []
    for j in (0, 1):
        for g in (0, 1):
            ci = 2 * g + j
            o_c = o_cmp[g * HEAD_DIM:(g + 1) * HEAD_DIM, ci * tq:(ci + 1) * tq]
            y = 0.0
            for br, o in enumerate((o_c, o_sel[ci], o_win[ci])):
                r0 = br * BRANCH_WIDTH + j * LANES + g * HEAD_DIM
                y = y + gate_t[r0:r0 + HEAD_DIM] * o
            parts.append(y)
    o_ref[0] = jnp.concatenate(parts, axis=0).T


def _nsa_prompt(qn, qr, gate, kc, sel, win):
    b, t, _ = qn.shape
    tq, tk = min(ATT_TQ, t), min(NSA_TK, t)
    assert tq == LANES and tk // SEL_BLOCK <= SUBLANES
    assert tk % SEL_BLOCK == 0
    ns = t // SEL_BLOCK
    n_sel = min(N_SELECT, ns)
    n_tiles = t // tk
    qspec = lambda w: pl.BlockSpec((1, tq, w), lambda bi, i: (bi, i, 0))
    seq = lambda n, w: pl.BlockSpec((1, n, w), lambda bi, i: (bi, 0, 0))
    tiles = lambda r, c: pltpu.VMEM((n_tiles, r, c), BF16)
    return pl.pallas_call(
        functools.partial(_nsa_prompt_kernel, tq=tq, tk=tk, n_sel=n_sel, ns=ns),
        grid=(b, t // tq),
        in_specs=[qspec(256), qspec(256), qspec(768), seq(kc.shape[1], 256), seq(t, 256), seq(t, 256)],
        out_specs=qspec(256),
        out_shape=jax.ShapeDtypeStruct((b, t, BRANCH_WIDTH), F32),
        scratch_shapes=[tiles(tk, NSA_KV_WIDTH), tiles(NSA_KV_WIDTH, tk), tiles(tk, NSA_KV_WIDTH),
                        tiles(NSA_KV_WIDTH, tk),
                        pltpu.VMEM((LANES, NSA_KV_WIDTH), BF16), pltpu.VMEM((NSA_KV_WIDTH, LANES), BF16),
                        pltpu.VMEM((2, LANES, tq), F32), pltpu.VMEM((n_tiles, SUBLANES, 2 * tq), F32)],
        compiler_params=_params("parallel", "arbitrary"),
        name="nsa_prompt",
    )(qn, qr, gate, kc, sel, win)


def _compressed_attend(qn, kc, tvec):
    nc = kc.shape[0]
    pad = jnp.zeros((LANES - nc, NSA_KV_WIDTH), F32)
    kck = jnp.concatenate([kc[:, :NSA_KV_WIDTH], pad], axis=0).astype(BF16)
    kcv = jnp.concatenate([kc[:, NSA_KV_WIDTH:], pad], axis=0).astype(BF16)
    lane = lax.broadcasted_iota(jnp.int32, (1, LANES), 1)
    valid = (((lane + 1) * CMP_BLOCK - 1) <= tvec) & (lane < nc)
    s = jnp.where(valid, _dot_nt(qn, kck), NEG)
    e = jnp.exp(s - jnp.max(s, axis=-1, keepdims=True))
    any_valid = jnp.max(valid.astype(F32), axis=-1, keepdims=True)
    p = e / jnp.sum(e, axis=-1, keepdims=True) * any_valid
    return p, _dot(p.astype(BF16), kcv)


def _softmax_scores(raw, masks, vts):
    scores = [jnp.where(msk, s, NEG) for s, msk in zip(raw, masks)]
    m = functools.reduce(jnp.maximum, [jnp.max(s, axis=-1, keepdims=True) for s in scores])
    es = [jnp.exp(s - m) for s in scores]
    l = sum(jnp.sum(e, axis=-1, keepdims=True) for e in es)
    o = sum(_dot_nt(e.astype(BF16), vt) for e, vt in zip(es, vts))
    return o / l


def _nsa_sample_kernel(pt_ref, qn_ref, qr_ref, gate_ref, kc_ref, seln_ref, win_ref, winn_ref, e_ref, *rest,
                       n_pages, nd, past, n_sel, nb):
    pages, (o_ref, wino_ref, newt_ref, impt_ref) = rest[:nb * n_pages], rest[nb * n_pages:]

    @pl.when(pl.program_id(0) == 0)
    def _():
        newt_ref[...] = jnp.zeros_like(newt_ref)

    rows = 4 * nd
    row = lax.broadcasted_iota(jnp.int32, (rows, 1), 0)
    lane = lax.broadcasted_iota(jnp.int32, (1, LANES), 1)
    group_mask = (_div(lane, HEAD_DIM) == _div(row, 2 * nd)).astype(F32)
    tau = _mod(row, nd)
    page = pages[0].shape[1]
    wb = win_ref.shape[2]
    assert page == LANES

    k_half = lambda x: x[:NSA_KV_WIDTH].astype(BF16)
    v_half = lambda x: x[NSA_KV_WIDTH:].astype(BF16)

    ps, o_cmps, o_wins, sel_raw, sel_vts = [], [], [], [], []
    for s in range(nb):
        newt_ref[s, 0, 0:nd, :] = seln_ref[s]
        newt_ref[s, 1, 0:nd, :] = winn_ref[s]
        qn = (qn_ref[s] * SCALE * group_mask).astype(BF16)
        qr = (qr_ref[s] * SCALE * group_mask).astype(BF16)
        p, o_cmp = _compressed_attend(qn, kc_ref[s], past + tau)
        ps.append(p)
        o_cmps.append(o_cmp)

        new_win_t = newt_ref[s, 1].T
        win_tiles = [win_ref[s, :, wi * page:(wi + 1) * page] for wi in range(wb // page)] + [new_win_t]
        masks = [(wi * page + lane > tau + (wb - WINDOW)) & (wi * page + lane >= wb - past)
                 for wi in range(wb // page)] + [lane <= tau]
        o_wins.append(_softmax_scores([_dot(qr, k_half(wt)) for wt in win_tiles], masks,
                                      [v_half(wt) for wt in win_tiles]))
        rolled = pltpu.roll(win_ref[s], wb - nd, 1)
        wino_ref[s, :, 0:wb - page] = rolled[:, 0:wb - page]
        wino_ref[s, :, wb - page:wb] = jnp.where(lane >= page - nd, pltpu.roll(new_win_t, page - nd, 1),
                                                 rolled[:, wb - page:wb])

        sel_tiles = [pages[s * n_pages + pi] for pi in range(n_pages)] + [newt_ref[s, 0].T]
        sel_raw.append([_dot(qr, k_half(st)) for st in sel_tiles])
        sel_vts.append(sel_tiles)

    p_all = jnp.concatenate(ps, axis=0)
    imp = p_all + pltpu.roll(p_all, nd, 0)
    impt_ref[...] = jnp.concatenate([imp, jnp.zeros((LANES - nb * rows, LANES), F32)], axis=0).T
    nbk = -(-((past + nd - 1) // SEL_BLOCK + 1) // SUBLANES) * SUBLANES
    pair = impt_ref[pl.ds(0, nbk, stride=2), :] + impt_ref[pl.ds(1, nbk, stride=2), :]
    sel_t = _rank_select(_block_scores(pair, past + _mod(lane, nd)), n_sel)
    sel_rn = jnp.concatenate([sel_t, jnp.zeros((LANES - nbk, LANES), F32)], axis=0).T[:nb * rows]
    row_all = lax.broadcasted_iota(jnp.int32, (nb * rows, 1), 0)
    sel4_all = jnp.where(_mod(_div(row_all, nd), 2) == 1, sel_rn,
                         pltpu.roll(sel_rn, nb * rows - nd, 0)).astype(BF16)

    for s in range(nb):
        sel4 = sel4_all[s * rows:(s + 1) * rows]
        masks = [_dot(sel4, e_ref[pi]) > 0.5 for pi in range(n_pages)]
        masks.append((_dot(sel4, e_ref[n_pages]) > 0.5) & (lane <= tau))
        o_sel = _softmax_scores(sel_raw[s], masks, [v_half(st) for st in sel_vts[s]])
        y = (gate_ref[s, 0] * o_cmps[s] + gate_ref[s, 1] * o_sel + gate_ref[s, 2] * o_wins[s]) * group_mask
        o_ref[s] = y[0:2 * nd] + y[2 * nd:4 * nd]


def _expand_matrix(n_tiles, tk):
    n = np.arange(LANES)[None, :, None]
    key = (np.arange(n_tiles)[:, None, None] * tk + np.arange(tk)[None, None, :])
    return jnp.asarray(n == key // SEL_BLOCK, BF16)


def _nsa_sample(page_table, qn16, qr16, gate16, kc, sel_new, win_state, win_new, cache_sel, layer, past):
    ns, n_pages = page_table.shape
    nd = sel_new.shape[1]
    assert nd == 4
    width, page = cache_sel.shape[2:]
    wb = win_state.shape[2]
    assert wb % page == 0 and page == LANES
    n_sel = min(N_SELECT, -(-(past + nd) // SEL_BLOCK))
    e = _expand_matrix(n_pages + 1, page)
    nb = math.gcd(ns, SAMPLE_SEQS_PER_STEP)
    per_seq = lambda *s: pl.BlockSpec((nb,) + s, lambda b, pt, n=len(s): (b,) + (0,) * n)
    grid_spec = pltpu.PrefetchScalarGridSpec(
        num_scalar_prefetch=1,
        grid=(ns // nb,),
        in_specs=[per_seq(16, LANES), per_seq(16, LANES), per_seq(3, 16, LANES), per_seq(kc.shape[1], 256),
                  per_seq(nd, 256), per_seq(256, wb), per_seq(nd, 256),
                  pl.BlockSpec(e.shape, lambda b, pt: (0, 0, 0))] + _page_specs(cache_sel, layer, n_pages, nb),
        out_specs=[per_seq(2 * nd, LANES), per_seq(256, wb)],
        scratch_shapes=[pltpu.VMEM((nb, 2, page, 256), F32), pltpu.VMEM((LANES, LANES), F32)],
    )
    return pl.pallas_call(
        functools.partial(_nsa_sample_kernel, n_pages=n_pages, nd=nd, past=past, n_sel=n_sel, nb=nb),
        grid_spec=grid_spec,
        out_shape=[jax.ShapeDtypeStruct((ns, 2 * nd, LANES), F32), jax.ShapeDtypeStruct((ns, 256, wb), F32)],
        compiler_params=_params("arbitrary"),
        name="nsa_sample",
    )(page_table, qn16, qr16, gate16, kc, sel_new, win_state, win_new, e, *([cache_sel] * (nb * n_pages)))


_SC_HALO, _CC_HALO = 8, 32


def _conv_kernel(u_ref, scb_ref, a_ref, uprev_ref, aprev_ref, stsc_ref, stcc_ref,
                 scw_ref, ccw_ref, ccb_ref, ccg_ref, ccbeta_ref,
                 yc_ref, yd_ref, nsc_ref, ncc_ref, fu_ref, fa_ref, *, tc):
    c = pl.program_id(1)

    @pl.when(c == 0)
    def _():
        fu_ref[:, _SC_HALO - (SC_CONV_LEN - 1):_SC_HALO, :] = stsc_ref[...]
        fa_ref[:, _CC_HALO - (CC_CONV_LEN - 1):_CC_HALO, :] = stcc_ref[...]

    @pl.when(c > 0)
    def _():
        fu_ref[:, 0:_SC_HALO, :] = uprev_ref[...]
        fa_ref[:, 0:_CC_HALO, :] = aprev_ref[...]

    fu_ref[:, _SC_HALO:, :] = u_ref[...]
    fa_ref[:, _CC_HALO:, :] = a_ref[...]
    sub = min(CONV_SUB, tc)
    for s in range(tc // sub):
        r0 = s * sub
        acc = 0.0
        for i in range(SC_CONV_LEN):
            o = r0 + i + _SC_HALO - (SC_CONV_LEN - 1)
            acc = acc + scw_ref[i:i + 1, :] * fu_ref[:, o:o + sub, :]
        yc_ref[:, r0:r0 + sub, :] = scb_ref[:, r0:r0 + sub, :] * acc
        acc = 0.0
        for i in range(CC_CONV_LEN):
            o = r0 + i + _CC_HALO - (CC_CONV_LEN - 1)
            acc = acc + ccw_ref[i:i + 1, :] * fa_ref[:, o:o + sub, :]
        z = acc + ccb_ref[...]
        mu = jnp.mean(z, axis=-1, keepdims=True)
        var = jnp.mean(jnp.square(z - mu), axis=-1, keepdims=True)
        ln = (z - mu) * lax.rsqrt(var + NORM_EPS) * ccg_ref[...] + ccbeta_ref[...]
        yd_ref[:, r0:r0 + sub, :] = ln * jax.nn.sigmoid(ln)
    nsc_ref[...] = fu_ref[:, tc + _SC_HALO - (SC_CONV_LEN - 1):tc + _SC_HALO, :]
    ncc_ref[...] = fa_ref[:, tc + _CC_HALO - (CC_CONV_LEN - 1):tc + _CC_HALO, :]


def _conv(u, scb, a, st_sc, st_cc, scw, ccw, ccb, ccg, ccbeta, nb):
    b, t, w = u.shape
    tc = min(CONV_TC, t)
    chunked = t > tc
    if chunked:
        uprev, aprev = u, a
        prev_u = pl.BlockSpec((nb, _SC_HALO, w), lambda bi, c: (bi, jnp.maximum(c * (tc // _SC_HALO) - 1, 0), 0))
        prev_a = pl.BlockSpec((nb, _CC_HALO, w), lambda bi, c: (bi, jnp.maximum(c * (tc // _CC_HALO) - 1, 0), 0))
    else:
        uprev, aprev = jnp.zeros((b, _SC_HALO, w), F32), jnp.zeros((b, _CC_HALO, w), F32)
        prev_u = pl.BlockSpec((nb, _SC_HALO, w), lambda bi, c: (bi, 0, 0))
        prev_a = pl.BlockSpec((nb, _CC_HALO, w), lambda bi, c: (bi, 0, 0))
    cur = pl.BlockSpec((nb, tc, w), lambda bi, c: (bi, c, 0))
    state = lambda n: pl.BlockSpec((nb, n, w), lambda bi, c: (bi, 0, 0))
    wspec = lambda arr: pl.BlockSpec(arr.shape, lambda bi, c: (0, 0))
    return pl.pallas_call(
        functools.partial(_conv_kernel, tc=tc),
        grid=(b // nb, t // tc),
        in_specs=[cur, cur, cur, prev_u, prev_a, state(SC_CONV_LEN - 1), state(CC_CONV_LEN - 1),
                  wspec(scw), wspec(ccw), wspec(ccb), wspec(ccg), wspec(ccbeta)],
        out_specs=[cur, cur, state(SC_CONV_LEN - 1), state(CC_CONV_LEN - 1)],
        out_shape=[jax.ShapeDtypeStruct((b, t, w), F32), jax.ShapeDtypeStruct((b, t, w), F32),
                   jax.ShapeDtypeStruct((b, SC_CONV_LEN - 1, w), F32),
                   jax.ShapeDtypeStruct((b, CC_CONV_LEN - 1, w), F32)],
        scratch_shapes=[pltpu.VMEM((nb, _SC_HALO + tc, w), F32), pltpu.VMEM((nb, _CC_HALO + tc, w), F32)],
        compiler_params=_params("parallel", "arbitrary"),
        name="conv",
    )(u, scb, a, uprev, aprev, st_sc, st_cc, scw, ccw, ccb, ccg, ccbeta)


def _merge_kernel(x_ref, ya_ref, yb_ref, yc_ref, yd_ref, g_ref, wg_ref, bg_ref, wb_ref, wo_ref, o_ref):
    x = x_ref[...]
    d = x.shape[-1]
    h = _rms(x, g_ref[...]).astype(BF16)
    merged = 0.0
    for n, y_ref in enumerate((ya_ref, yb_ref, yc_ref, yd_ref)):
        gate = jax.nn.sigmoid(_dot(h, wg_ref[:, n * d:(n + 1) * d]) + bg_ref[:, n * d:(n + 1) * d])
        merged = merged + gate * _dot(y_ref[...].astype(BF16), wb_ref[n])
    o_ref[...] = x + _dot(merged.astype(BF16), wo_ref[...])


def _merge(x, ya, yb, yc, yd, g, wg, bg, wb, wo):
    n, d = x.shape
    tm = min(TOKEN_TILE, n)
    tok = lambda w: pl.BlockSpec((tm, w), lambda i: (i, 0))
    return pl.pallas_call(
        _merge_kernel,
        grid=(n // tm,),
        in_specs=[tok(d)] + [tok(BRANCH_WIDTH)] * 4 + [_full(a.shape) for a in (g, wg, bg, wb, wo)],
        out_specs=tok(d),
        out_shape=jax.ShapeDtypeStruct((n, d), F32),
        compiler_params=_params("parallel"),
        name="merge",
    )(x, ya, yb, yc, yd, g, wg, bg, wb, wo)


def _regroup_w_in(w_in):
    splits = (BRANCH_WIDTH,) * 4 + (NSA_KV_WIDTH,) * 6 + (12,) + (BRANCH_WIDTH,) * 5
    offs = np.concatenate([[0], np.cumsum(splits)])
    o_nq, o_ck, o_ng, o_scb = offs[3], offs[4], offs[10], offs[11]
    heads_jg = [g * 2 + j for j in (0, 1) for g in (0, 1)]
    jg_heads = np.concatenate([np.arange(h * HEAD_DIM, (h + 1) * HEAD_DIM) for h in heads_jg])
    nq = [w_in[..., o_nq + h * HEAD_DIM: o_nq + (h + 1) * HEAD_DIM] for h in heads_jg]
    gate_cols = np.array([o_ng + g * 6 + j * 3 + br for br in range(3) for j in (0, 1) for g in (0, 1)])
    gates = jnp.repeat(w_in[..., gate_cols], HEAD_DIM, axis=-1)
    out = jnp.concatenate([w_in[..., :o_nq]] + nq + [w_in[..., o_ck:o_ng], gates, w_in[..., o_scb:]], axis=-1)
    assert out.shape[-1] == _P_END
    return out.astype(BF16), jg_heads


def _rope_tables(pos):
    half = HEAD_DIM // 2
    inv_freq = jnp.exp(-math.log(ROPE_THETA) * jnp.arange(half, dtype=F32) / half)
    ang = pos.astype(F32)[:, None] * inv_freq[None, :]
    cos, sin = jnp.cos(ang), jnp.sin(ang)
    reps = BRANCH_WIDTH // HEAD_DIM
    return jnp.tile(jnp.concatenate([cos, cos], -1), (1, reps)), jnp.tile(jnp.concatenate([-sin, sin], -1), (1, reps))


def _rows16(a, nd):
    ns = a.shape[0]
    a = a.reshape(ns, nd, 2, LANES).transpose(0, 2, 1, 3)
    return jnp.broadcast_to(a[:, None], (ns, 2, 2, nd, LANES)).reshape(ns, 4 * nd, LANES)


def _feature_major(cache):
    l, p, rows = cache.shape[:3]
    return cache.transpose(0, 1, 3, 4, 5, 2).reshape(l, p, -1, rows)


def kernel(x_prompt, x_sample, cache_sb_kv, cache_cmp_kv, cache_sel_kv, state_win_kv, state_conv_sc,
           state_conv_cc, page_table, norm_ffn1, w_ffn1_gu, w_ffn1_down, norm_mix, w_in, nsa_q_gain,
           nsa_k_gain, nsa_cmp_pe, nsa_cmp_w, sc_conv_w, cc_conv_w, cc_conv_b, cc_norm_g, cc_norm_b,
           w_branch, w_gate, b_gate, w_out, norm_ffn2, w_ffn2_gu, w_ffn2_down):
    b, t, d = x_prompt.shape
    ns, nd, _ = x_sample.shape
    depth, n_phys, page = cache_sb_kv.shape[:3]
    n_pages = page_table.shape[1]
    past = n_pages * page
    assert w_ffn1_down.shape[1] % FFN_CHUNK == 0
    wb = state_win_kv.shape[2]

    wgu1, wgu2 = w_ffn1_gu.astype(BF16), w_ffn2_gu.astype(BF16)
    wd1, wd2 = w_ffn1_down.astype(BF16), w_ffn2_down.astype(BF16)
    w_in_r, jg_heads = _regroup_w_in(w_in)
    w_gate_b = w_gate.astype(BF16)
    w_branch_b = w_branch.at[:, 1].set(w_branch[:, 1][:, jg_heads, :]).astype(BF16)
    w_out_b = w_out.astype(BF16)
    eye2 = jnp.eye(2, dtype=F32)
    w_cmp = jnp.einsum("Lklde,gG->LklgdGe", nsa_cmp_w, eye2).reshape(depth, 2, CMP_BLOCK, 128, 128)
    w_cmp = w_cmp.astype(BF16)
    pe_rows = jnp.tile(nsa_cmp_pe, (1, 1, 1, 2))
    row2 = lambda a: a.reshape(depth, 1, -1)
    n1, nm, n2, bg = row2(norm_ffn1), row2(norm_mix), row2(norm_ffn2), row2(b_gate)
    qg = jnp.tile(row2(nsa_q_gain), (1, 1, 4))
    kg = jnp.tile(row2(nsa_k_gain), (1, 1, 2))
    ccb, ccg, ccbeta = row2(cc_conv_b), row2(cc_norm_g), row2(cc_norm_b)
    hd = np.arange(BRANCH_WIDTH) // HEAD_DIM
    bd = jnp.asarray(hd[:, None] == hd[None, :], BF16)
    bd1 = bd[:LANES, :LANES]
    cos_p, sin_p = _rope_tables(jnp.arange(t, dtype=jnp.int32))
    cos_s, sin_s = _rope_tables(past + jnp.arange(ns * nd, dtype=jnp.int32) % nd)

    cache_sb = _feature_major(cache_sb_kv)
    cache_cmp = _feature_major(cache_cmp_kv)
    cache_sel = _feature_major(cache_sel_kv)
    win_state = _feature_major(state_win_kv)

    xp = x_prompt.reshape(b * t, d)
    xs = x_sample.reshape(ns * nd, d)
    zeros_sc = jnp.zeros((b, SC_CONV_LEN - 1, BRANCH_WIDTH), F32)
    zeros_cc = jnp.zeros((b, CC_CONV_LEN - 1, BRANCH_WIDTH), F32)
    outs_p = [[] for _ in range(6)]
    outs_s = [[] for _ in range(6)]
    for l in range(depth):
        xp = _ffn(xp, n1[l], wgu1[l], wd1[l])
        xs = _ffn(xs, n1[l], wgu1[l], wd1[l])

        (sbq, sbkv, qn, qr, cmp_, sel, win, gate, scb, scu, cca) = _proj(
            xp, nm[l], w_in_r[l], cos_p, sin_p, qg[l], kg[l], bd)
        seq = lambda a: a.reshape(b, t, a.shape[-1])
        ya = _sb_prompt(seq(sbq), seq(sbkv))
        kc = _compress_prompt(seq(cmp_), pe_rows[l], w_cmp[l], kg[l], bd1)
        yb = _nsa_prompt(seq(qn), seq(qr), seq(gate), kc, seq(sel), seq(win))
        yc, yd, nsc, ncc = _conv(seq(scu), seq(scb), seq(cca), zeros_sc, zeros_cc, sc_conv_w[l], cc_conv_w[l],
                                 ccb[l], ccg[l], ccbeta[l], nb=1)
        flat = lambda a: a.reshape(b * t, a.shape[-1])
        xp = _merge(xp, flat(ya), flat(yb), flat(yc), flat(yd), nm[l], w_gate_b[l], bg[l], w_branch_b[l],
                    w_out_b[l])
        for lst, a in zip(outs_p, (seq(sbkv), seq(cmp_), seq(sel), seq(win)[:, t - min(WINDOW, t):], nsc, ncc)):
            lst.append(a)

        (sbq, sbkv, qn, qr, cmp_, sel, win, gate, scb, scu, cca) = _proj(
            xs, nm[l], w_in_r[l], cos_s, sin_s, qg[l], kg[l], bd)
        seq = lambda a: a.reshape(ns, nd, a.shape[-1])
        q16 = jnp.tile(seq(sbq), (1, 4, 1))
        ya = _sb_sample(page_table, q16, seq(sbkv), cache_sb, l)
        kc = _compress_sample(page_table, cache_cmp, l, pe_rows[l], w_cmp[l], kg[l], bd1)
        gate16 = jnp.stack([_rows16(seq(gate)[:, :, br * 256:(br + 1) * 256], nd) for br in range(3)], axis=1)
        y8, win_out = _nsa_sample(page_table, _rows16(seq(qn), nd), _rows16(seq(qr), nd), gate16, kc,
                                  seq(sel), win_state[l], seq(win), cache_sel, l, past)
        yb = y8.reshape(ns, 2, nd, LANES).transpose(0, 2, 1, 3).reshape(ns, nd, BRANCH_WIDTH)
        yc, yd, nsc, ncc = _conv(seq(scu), seq(scb), seq(cca), state_conv_sc[l], state_conv_cc[l],
                                 sc_conv_w[l], cc_conv_w[l], ccb[l], ccg[l], ccbeta[l], nb=min(8, ns))
        flat = lambda a: a.reshape(ns * nd, a.shape[-1])
        xs = _merge(xs, flat(ya), flat(yb), flat(yc), flat(yd), nm[l], w_gate_b[l], bg[l], w_branch_b[l],
                    w_out_b[l])
        for lst, a in zip(outs_s, (seq(sbkv), seq(cmp_), seq(sel), win_out, nsc, ncc)):
            lst.append(a)

        xp = _ffn(xp, n2[l], wgu2[l], wd2[l])
        xs = _ffn(xs, n2[l], wgu2[l], wd2[l])

    def kv5(a, heads):
        a = jnp.stack(a)
        return a.reshape(a.shape[:3] + (2, heads, HEAD_DIM))

    sb_heads, nsa_kv_heads = BRANCH_WIDTH // HEAD_DIM, NSA_KV_WIDTH // HEAD_DIM
    win_s = jnp.stack(outs_s[3])
    win_s = win_s.reshape(depth, ns, 2, nsa_kv_heads, HEAD_DIM, wb).transpose(0, 1, 5, 2, 3, 4)
    return (xp.reshape(b, t, d), xs.reshape(ns, nd, d),
            kv5(outs_p[0], sb_heads), kv5(outs_s[0], sb_heads),
            kv5(outs_p[1], nsa_kv_heads), kv5(outs_s[1], nsa_kv_heads),
            kv5(outs_p[2], nsa_kv_heads), kv5(outs_s[2], nsa_kv_heads),
            kv5(outs_p[3], nsa_kv_heads), win_s,
            jnp.stack(outs_p[4]), jnp.stack(outs_s[4]), jnp.stack(outs_p[5]), jnp.stack(outs_s[5]))
```

```python
import functools
import math

import numpy as np
import jax
import jax.numpy as jnp
from jax import lax
from jax.experimental import pallas as pl
from jax.experimental.pallas import tpu as pltpu

F32, BF16 = jnp.float32, jnp.bfloat16

HEAD_DIM = 64
BRANCH_WIDTH = 256
N_BRANCH = 4
NSA_KV_WIDTH = 128
CMP_BLOCK = 32
SEL_BLOCK = 64
N_SELECT = 16
WINDOW = 512
SC_CONV_LEN = 3
CC_CONV_LEN = 31
ROPE_THETA = 10000.0
NORM_EPS = 1e-6
NEG = -1e30
SCALE = HEAD_DIM ** -0.5

LANES = 128
SUBLANES = 8
VMEM_LIMIT_BYTES = 56 * 1024 * 1024
TOKEN_TILE = 512
FFN_CHUNK = 256
ATT_TQ = 128
SB_TK = 256
NSA_TK = 512
SAMPLE_SEQS_PER_STEP = 4
CMP_PITCH = CMP_BLOCK + 1
CONV_TC = 256
CONV_SUB = 64

_NT = (((1,), (1,)), ((), ()))


def _params(*sem):
    return pltpu.CompilerParams(dimension_semantics=sem, vmem_limit_bytes=VMEM_LIMIT_BYTES)


def _full(shape):
    n = len(shape)
    return pl.BlockSpec(shape, lambda *_, n=n: (0,) * n)


def _layer(arr, layer):
    n = arr.ndim - 1
    return pl.BlockSpec((None,) + arr.shape[1:], lambda *_, n=n: (layer,) + (0,) * n)


def _dot(a, b):
    return jnp.dot(a, b, preferred_element_type=F32)


def _dot_nt(a, b):
    return lax.dot_general(a, b, _NT, preferred_element_type=F32)


def _div(x, n):
    assert n & (n - 1) == 0
    return x >> (n.bit_length() - 1)


def _mod(x, n):
    assert n & (n - 1) == 0
    return x & (n - 1)


def _rms(x, g):
    return x * lax.rsqrt(jnp.mean(x * x, axis=-1, keepdims=True) + NORM_EPS) * g


def _split(x):
    hi = x.astype(BF16)
    return hi, (x - hi.astype(F32)).astype(BF16)


def _split_dot(x, m):
    hi, lo = _split(x)
    return _dot(hi, m) + _dot(lo, m)


def _head_rms(x, bd, g):
    ms = _split_dot(x * x, bd) * (1.0 / HEAD_DIM)
    return x * lax.rsqrt(ms + NORM_EPS) * g


def _rope(x, cos, sin_signed):
    w = x.shape[-1]
    lane = lax.broadcasted_iota(jnp.int32, x.shape, 1)
    first_half = (lane & (HEAD_DIM - 1)) < HEAD_DIM // 2
    rot = jnp.where(first_half, pltpu.roll(x, w - HEAD_DIM // 2, 1), pltpu.roll(x, HEAD_DIM // 2, 1))
    return x * cos + rot * sin_signed


def _ffn_kernel(x_ref, g_ref, wgu_ref, wd_ref, o_ref, h_ref, acc_ref):
    f = wd_ref.shape[0]
    h_ref[...] = _rms(x_ref[...], g_ref[...]).astype(BF16)
    for c in range(f // FFN_CHUNK):
        cols = slice(c * FFN_CHUNK, (c + 1) * FFN_CHUNK)
        h = h_ref[...]
        g = _dot(h, wgu_ref[:, cols])
        u = _dot(h, wgu_ref[:, f + c * FFN_CHUNK: f + (c + 1) * FFN_CHUNK])
        a = (g * jax.nn.sigmoid(g) * u).astype(BF16)
        down = _dot(a, wd_ref[cols, :])
        if c == 0:
            acc_ref[...] = down
        else:
            acc_ref[...] += down
    o_ref[...] = x_ref[...] + 0.5 * acc_ref[...]


def _ffn(x, g, wgu, wd, layer):
    n, d = x.shape
    tm = min(TOKEN_TILE, n)
    return pl.pallas_call(
        _ffn_kernel,
        grid=(n // tm,),
        in_specs=[pl.BlockSpec((tm, d), lambda i: (i, 0)), _layer(g, layer), _layer(wgu, layer),
                  _layer(wd, layer)],
        out_specs=pl.BlockSpec((tm, d), lambda i: (i, 0)),
        out_shape=jax.ShapeDtypeStruct((n, d), F32),
        scratch_shapes=[pltpu.VMEM((tm, d), BF16), pltpu.VMEM((tm, d), F32)],
        compiler_params=_params("parallel"),
        name="ffn",
    )(x, g, wgu, wd)


_P_SBQ, _P_SBKV, _P_NQ, _P_CMP, _P_SEL, _P_WIN, _P_GATE = 0, 256, 768, 1024, 1280, 1536, 1792
_P_SCB, _P_SCC, _P_SCH, _P_CCA, _P_CCG, _P_END = 2560, 2816, 3072, 3328, 3584, 3840


def _proj_kernel(x_ref, g_ref, w_ref, cos_ref, sin_ref, qg_ref, kg_ref, bd_ref, *rest, n_stacked):
    (sbq_ref, sbkv_ref, qn_ref, qr_ref, cmp_ref, sel_ref, win_ref, gate_ref,
     scb_ref, scu_ref, cca_ref) = rest[n_stacked:n_stacked + 11]
    h = _rms(x_ref[...], g_ref[...]).astype(BF16)

    def mm(a, b):
        return _dot(h, w_ref[:, a:b])

    cos, sin, bd = cos_ref[...], sin_ref[...], bd_ref[...]
    sbq_ref[...] = mm(_P_SBQ, _P_SBKV)
    sbkv_ref[...] = mm(_P_SBKV, _P_NQ)
    qn = _head_rms(mm(_P_NQ, _P_CMP), bd, qg_ref[...])
    qn_ref[...] = qn
    qr_ref[...] = _rope(qn, cos, sin)
    cmp_ref[...] = mm(_P_CMP, _P_SEL)
    bd1, cos1, sin1 = bd[:LANES, :LANES], cos[:, :LANES], sin[:, :LANES]
    for a, ref in ((_P_SEL, sel_ref), (_P_WIN, win_ref)):
        kv = mm(a, a + 2 * NSA_KV_WIDTH)
        ref[:, :NSA_KV_WIDTH] = _rope(_head_rms(kv[:, :NSA_KV_WIDTH], bd1, kg_ref[...]), cos1, sin1)
        ref[:, NSA_KV_WIDTH:] = kv[:, NSA_KV_WIDTH:]
    gate_ref[...] = jax.nn.sigmoid(mm(_P_GATE, _P_SCB))
    scb_ref[...] = mm(_P_SCB, _P_SCC)
    scu_ref[...] = mm(_P_SCC, _P_SCH) * mm(_P_SCH, _P_CCA)
    cca_ref[...] = mm(_P_CCA, _P_CCG) * jax.nn.sigmoid(mm(_P_CCG, _P_END))
    for src, dst in zip((sbkv_ref, cmp_ref, sel_ref, win_ref), rest[n_stacked + 11:]):
        dst[0] = src[...].T


def _proj(x, g, w, cos, sin, qg, kg, bd, layer, stacked=()):
    n, d = x.shape
    tm = min(TOKEN_TILE, n)
    n_rope_tiles = cos.shape[0] // tm
    widths = (256, 512, 256, 256, 256, 256, 256, 768, 256, 256, 256)
    tok = lambda wd_: pl.BlockSpec((tm, wd_), lambda i: (i, 0))
    rope = pl.BlockSpec((tm, cos.shape[1]), lambda i: (i % n_rope_tiles, 0))
    out_specs = [tok(wd_) for wd_ in widths]
    out_shape = [jax.ShapeDtypeStruct((n, wd_), F32) for wd_ in widths]
    for buf in stacked:
        tiles_per_seq = n // buf.shape[1] // tm
        if buf.shape[3] == tiles_per_seq * tm:
            spec = pl.BlockSpec((None, 1, buf.shape[2], tm),
                                lambda i, tps=tiles_per_seq: (layer, i // tps, 0, i % tps))
        else:
            assert buf.shape[3] == tm
            spec = pl.BlockSpec((None, 1, buf.shape[2], tm), lambda i, tps=tiles_per_seq: (layer, i // tps, 0, 0))
        out_specs.append(spec)
        out_shape.append(jax.ShapeDtypeStruct(buf.shape, F32))
    n_in = 8
    outs = pl.pallas_call(
        functools.partial(_proj_kernel, n_stacked=len(stacked)),
        grid=(n // tm,),
        in_specs=[tok(d), _layer(g, layer), _layer(w, layer), rope, rope, _layer(qg, layer), _layer(kg, layer),
                  _full(bd.shape)] + [pl.BlockSpec(memory_space=pl.ANY)] * len(stacked),
        out_specs=out_specs,
        out_shape=out_shape,
        input_output_aliases={n_in + k: len(widths) + k for k in range(len(stacked))},
        compiler_params=_params("arbitrary"),
        name="proj",
    )(x, g, w, cos, sin, qg, kg, bd, *stacked)
    return outs[:len(widths)], outs[len(widths):]


def _sb_logs(z):
    log_beta = jnp.minimum(z, 0.0) - jnp.log(1.0 + jnp.exp(-jnp.abs(z)))
    return log_beta, log_beta - z


def _sb_tile_t(k, vts, q_ts, valid, carry, tri_t):
    logs = [_sb_logs(_dot(k, q)) for q in q_ts]
    keeps = [lk if valid is None else jnp.where(valid, lk, 0.0) for _, lk in logs]
    between = [_dot(tri_t, hi) + _dot(tri_t, lo) for hi, lo in [_split(lk) for lk in keeps]]
    ws = [jnp.exp(lb + bt + c) for (lb, _), bt, (c, _) in zip(logs, between, carry)]
    if valid is not None:
        ws = [jnp.where(valid, w, 0.0) for w in ws]
    pvs = [_dot(vt, w.astype(BF16)) for vt, w in zip(vts, ws)]
    return tuple((c + bt[0:1] + lk[0:1], acc + pv)
                 for (c, acc), bt, lk, pv in zip(carry, between, keeps, pvs))


def _sb_prompt_kernel(q_ref, kv_ref, tri_ref, o_ref, kb_ref, vtb_ref, *, tq, tk):
    i = pl.program_id(1)
    n_all = kb_ref.shape[0]

    @pl.when(i == 0)
    def _():
        for j in range(n_all):
            tile = kv_ref[0, j * tk:(j + 1) * tk, :]
            kb_ref[j] = tile[:, :BRANCH_WIDTH].astype(BF16)
            vtb_ref[j] = tile[:, BRANCH_WIDTH:].T.astype(BF16)

    t0 = i * tq
    heads = BRANCH_WIDTH // HEAD_DIM
    feat = lax.broadcasted_iota(jnp.int32, (BRANCH_WIDTH, 1), 0)
    q_feat = q_ref[0].T * SCALE
    q_h = [jnp.where(_div(feat, HEAD_DIM) == h, q_feat, 0.0).astype(BF16) for h in range(heads)]
    tvec = t0 + lax.broadcasted_iota(jnp.int32, (1, tq), 1)
    tri_t = tri_ref[...]
    j_diag = (t0 + tq - 1) // tk
    kpos = j_diag * tk + lax.broadcasted_iota(jnp.int32, (tk, 1), 0)

    def tile(j, carry, valid):
        vts = [vtb_ref[j, h * HEAD_DIM:(h + 1) * HEAD_DIM, :] for h in range(heads)]
        return _sb_tile_t(kb_ref[j], vts, q_h, valid, carry, tri_t)

    init = tuple((jnp.zeros((1, tq), F32), jnp.zeros((HEAD_DIM, tq), F32)) for _ in range(heads))
    carry = tile(j_diag, init, kpos < tvec)
    carry = lax.fori_loop(0, j_diag, lambda it, carry: tile(j_diag - 1 - it, carry, None), carry)
    o_ref[0] = jnp.concatenate([acc for _, acc in carry], axis=0).T


def _tri(n, upper):
    r = np.arange(n)
    m = r[:, None] < r[None, :] if upper else r[:, None] > r[None, :]
    return jnp.asarray(m, BF16)


def _sb_prompt(q, kv):
    b, t, _ = q.shape
    tq, tk = min(ATT_TQ, t), min(SB_TK, t)
    assert tk % tq == 0
    return pl.pallas_call(
        functools.partial(_sb_prompt_kernel, tq=tq, tk=tk),
        grid=(b, t // tq),
        in_specs=[pl.BlockSpec((1, tq, BRANCH_WIDTH), lambda bi, i: (bi, i, 0)),
                  pl.BlockSpec((1, t, 2 * BRANCH_WIDTH), lambda bi, i: (bi, 0, 0)),
                  _full((tk, tk))],
        out_specs=pl.BlockSpec((1, tq, BRANCH_WIDTH), lambda bi, i: (bi, i, 0)),
        out_shape=jax.ShapeDtypeStruct((b, t, BRANCH_WIDTH), F32),
        scratch_shapes=[pltpu.VMEM((t // tk, tk, BRANCH_WIDTH), BF16),
                        pltpu.VMEM((t // tk, BRANCH_WIDTH, tk), BF16)],
        compiler_params=_params("parallel", "arbitrary"),
        name="sb_prompt",
    )(q, kv, _tri(tk, upper=True))


def _sb_sample_kernel(pt_ref, q_ref, new_ref, tri_ref, *rest, n_pages, nd, nb):
    pages, (o_ref, newt_ref) = rest[:nb * n_pages], rest[nb * n_pages:]

    @pl.when(pl.program_id(0) == 0)
    def _():
        newt_ref[...] = jnp.zeros_like(newt_ref)

    rows = 4 * nd
    row = lax.broadcasted_iota(jnp.int32, (rows, 1), 0)
    lane = lax.broadcasted_iota(jnp.int32, (1, BRANCH_WIDTH), 1)
    head_mask = (_div(lane, HEAD_DIM) == _div(row, nd)).astype(F32)
    tri = tri_ref[...]
    key = lax.broadcasted_iota(jnp.int32, (1, newt_ref.shape[1]), 1)
    kv_half = lambda x, v: x[v * BRANCH_WIDTH:(v + 1) * BRANCH_WIDTH].astype(BF16)
    qms, tiles = [], []
    for s in range(nb):
        newt_ref[s, 0:nd, :] = new_ref[s]
        qms.append((q_ref[s] * SCALE * head_mask).astype(BF16))
        new_t = newt_ref[s].T
        seq_tiles = [(functools.partial(kv_half, new_t), key < _mod(row, nd))]
        for p in reversed(range(n_pages)):
            page_ref = pages[s * n_pages + p]
            seq_tiles.append((lambda v, r=page_ref: kv_half(r, v), None))
        tiles.append(seq_tiles)
    logs = [[_sb_logs(_dot(qm, get(0))) for get, _ in seq] for qm, seq in zip(qms, tiles)]
    keeps = [[lk if valid is None else jnp.where(valid, lk, 0.0) for (_, lk), (_, valid) in zip(lg, seq)]
             for lg, seq in zip(logs, tiles)]
    between = [[_split_dot(lk, tri) for lk in ks] for ks in keeps]
    weights = []
    for lg, ks, bt, seq in zip(logs, keeps, between, tiles):
        c = jnp.zeros((rows, 1), F32)
        ws = []
        for (lb, _), lk, b, (_, valid) in zip(lg, ks, bt, seq):
            w = jnp.exp(lb + b + c)
            ws.append((w if valid is None else jnp.where(valid, w, 0.0)).astype(BF16))
            c = c + jnp.sum(lk, axis=-1, keepdims=True)
        weights.append(ws)
    for s in range(nb):
        acc = sum(_dot_nt(w, get(1)) for w, (get, _) in zip(weights[s], tiles[s]))
        m = acc * head_mask
        hs = m[0:8] + m[8:16]
        o_ref[s] = hs + pltpu.roll(hs, 4, 0)


def _page_specs(cache, layer, n_pages, nb):
    width, page = cache.shape[2:]
    return [pl.BlockSpec((None, None, width, page), lambda b, pt, s=s, p=p: (layer, pt[b * nb + s, p], 0, 0))
            for s in range(nb) for p in range(n_pages)]


def _sb_sample(page_table, q16, new, cache, layer):
    ns, n_pages = page_table.shape
    nd = new.shape[1]
    assert nd == 4 and q16.shape[1] == 16
    width, page = cache.shape[2:]
    nb = math.gcd(ns, SAMPLE_SEQS_PER_STEP)
    grid_spec = pltpu.PrefetchScalarGridSpec(
        num_scalar_prefetch=1,
        grid=(ns // nb,),
        in_specs=[pl.BlockSpec((nb, 16, BRANCH_WIDTH), lambda b, pt: (b, 0, 0)),
                  pl.BlockSpec((nb, nd, width), lambda b, pt: (b, 0, 0)),
                  pl.BlockSpec((page, page), lambda b, pt: (0, 0))] + _page_specs(cache, layer, n_pages, nb),
        out_specs=pl.BlockSpec((nb, 8, BRANCH_WIDTH), lambda b, pt: (b, 0, 0)),
        scratch_shapes=[pltpu.VMEM((nb, page, width), F32)],
    )
    out = pl.pallas_call(
        functools.partial(_sb_sample_kernel, n_pages=n_pages, nd=nd, nb=nb),
        grid_spec=grid_spec,
        out_shape=jax.ShapeDtypeStruct((ns, 8, BRANCH_WIDTH), F32),
        compiler_params=_params("arbitrary"),
        name="sb_sample",
    )(page_table, q16, new, _tri(page, upper=False), *([cache] * (nb * n_pages)))
    return out[:, :nd]


def _compress_math(x_ref, nc, pitch, pe_ref, w_ref, kg, bd1):
    outs = []
    for kv in range(2):
        acc = jnp.zeros((nc, NSA_KV_WIDTH), F32)
        for l in range(CMP_BLOCK):
            xl = x_ref[kv, pl.ds(l, nc, stride=pitch), :] + pe_ref[kv, l:l + 1, :]
            acc = acc + _dot(xl.astype(BF16), w_ref[kv, l])
        outs.append(acc)
    return _head_rms(outs[0], bd1, kg), outs[1]


def _compress_prompt_kernel(x_ref, pe_ref, w_ref, kg_ref, bd_ref, o_ref, xs_ref):
    xs_ref[0] = x_ref[0, :, :NSA_KV_WIDTH]
    xs_ref[1] = x_ref[0, :, NSA_KV_WIDTH:]
    k, v = _compress_math(xs_ref, o_ref.shape[1], CMP_BLOCK, pe_ref, w_ref, kg_ref[...], bd_ref[...])
    o_ref[0, :, :NSA_KV_WIDTH] = k
    o_ref[0, :, NSA_KV_WIDTH:] = v


def _compress_prompt(x, pe, w, kg, bd1, layer):
    b, t, _ = x.shape
    nc = t // CMP_BLOCK
    return pl.pallas_call(
        _compress_prompt_kernel,
        grid=(b,),
        in_specs=[pl.BlockSpec((1, t, 2 * NSA_KV_WIDTH), lambda i: (i, 0, 0)), _layer(pe, layer),
                  _layer(w, layer), _layer(kg, layer), _full(bd1.shape)],
        out_specs=pl.BlockSpec((1, nc, 2 * NSA_KV_WIDTH), lambda i: (i, 0, 0)),
        out_shape=jax.ShapeDtypeStruct((b, nc, 2 * NSA_KV_WIDTH), F32),
        scratch_shapes=[pltpu.VMEM((2, t, NSA_KV_WIDTH), F32)],
        compiler_params=_params("parallel"),
        name="compress_prompt",
    )(x, pe, w, kg, bd1)


def _compress_sample_kernel(pt_ref, pe_ref, w_ref, kg_ref, bd_ref, *rest, n_pages, nb):
    pages, (o_ref, x_ref) = rest[:nb * n_pages], rest[nb * n_pages:]
    per_page = pages[0].shape[1] // CMP_BLOCK
    nc = o_ref.shape[1]
    for s in range(nb):
        for p in range(n_pages):
            for kv in range(2):
                rows = pages[s * n_pages + p][kv * NSA_KV_WIDTH:(kv + 1) * NSA_KV_WIDTH, :].T
                for q in range(per_page):
                    n = s * nc + p * per_page + q
                    x_ref[kv, pl.ds(n * CMP_PITCH, CMP_BLOCK), :] = rows[q * CMP_BLOCK:(q + 1) * CMP_BLOCK]
    k, v = _compress_math(x_ref, nb * nc, CMP_PITCH, pe_ref, w_ref, kg_ref[...], bd_ref[...])
    for s in range(nb):
        o_ref[s, :, :NSA_KV_WIDTH] = k[s * nc:(s + 1) * nc]
        o_ref[s, :, NSA_KV_WIDTH:] = v[s * nc:(s + 1) * nc]


def _compress_sample(page_table, cache, layer, pe, w, kg, bd1):
    ns, n_pages = page_table.shape
    width, page = cache.shape[2:]
    nc = n_pages * page // CMP_BLOCK
    nb = math.gcd(ns, SAMPLE_SEQS_PER_STEP)
    const = lambda shape: pl.BlockSpec(shape, lambda b, pt, n=len(shape): (0,) * n)
    grid_spec = pltpu.PrefetchScalarGridSpec(
        num_scalar_prefetch=1,
        grid=(ns // nb,),
        in_specs=[_layer(pe, layer), _layer(w, layer), _layer(kg, layer), const(bd1.shape)]
        + _page_specs(cache, layer, n_pages, nb),
        out_specs=pl.BlockSpec((nb, nc, 2 * NSA_KV_WIDTH), lambda b, pt: (b, 0, 0)),
        scratch_shapes=[pltpu.VMEM((2, -(-nb * nc * CMP_PITCH // SUBLANES) * SUBLANES, NSA_KV_WIDTH), F32)],
    )
    return pl.pallas_call(
        functools.partial(_compress_sample_kernel, n_pages=n_pages, nb=nb),
        grid_spec=grid_spec,
        out_shape=jax.ShapeDtypeStruct((ns, nc, 2 * NSA_KV_WIDTH), F32),
        compiler_params=_params("arbitrary"),
        name="compress_sample",
    )(page_table, pe, w, kg, bd1, *([cache] * (nb * n_pages)))


def _flash_t(q_ts, k_ref, vt_ref, lo, hi, tk, mask_fn):
    tq = q_ts[0].shape[1]

    def body(j, carry):
        k, vt = k_ref[j], vt_ref[j]
        masks = mask_fn(j, j * tk + lax.broadcasted_iota(jnp.int32, (tk, 1), 0))
        ss = [jnp.where(masks[ci // 2], _dot(k, q), NEG) for ci, q in enumerate(q_ts)]
        m2s = [jnp.maximum(m, jnp.max(s, axis=0, keepdims=True)) for s, (m, _, _) in zip(ss, carry)]
        ps = [jnp.exp(s - m2) for s, m2 in zip(ss, m2s)]
        pvs = [_dot(vt[(ci // 2) * HEAD_DIM:(ci // 2 + 1) * HEAD_DIM], p.astype(BF16)) for ci, p in enumerate(ps)]
        out = []
        for (m, l, acc), m2, p, pv in zip(carry, m2s, ps, pvs):
            a = jnp.exp(m - m2)
            out.append((m2, a * l + jnp.sum(p, axis=0, keepdims=True), a * acc + pv))
        return tuple(out)

    init = tuple((jnp.full((1, tq), NEG, F32), jnp.zeros((1, tq), F32), jnp.zeros((HEAD_DIM, tq), F32))
                 for _ in q_ts)
    return [acc / l for _, l, acc in lax.fori_loop(lo, hi, body, init)]


def _rank_select(score, n_sel):
    nb = score.shape[0]
    row = lax.broadcasted_iota(jnp.int32, (nb, 1), 0)
    ahead = jnp.zeros(score.shape, F32)
    for m in range(nb):
        sm = score[m:m + 1, :]
        ahead = ahead + jnp.where((sm > score) | ((sm == score) & (row > m)), 1.0, 0.0)
    return (ahead < n_sel).astype(F32)


def _block_scores(pair, tvec):
    n = lax.broadcasted_iota(jnp.int32, (pair.shape[0], 1), 0)
    cur = _div(tvec, SEL_BLOCK)
    forced = (n == 0) | (n == cur) | (n == cur - 1)
    return jnp.where(forced, 1e30, jnp.where(n * SEL_BLOCK <= tvec, pair, -1.0))


def _nsa_prompt_kernel(qn_ref, qr_ref, gate_ref, kc_ref, sel_ref, win_ref, o_ref,
                       selk_ref, selvt_ref, wink_ref, winvt_ref, kck_ref, kcvt_ref, imp_ref, sel8_ref,
                       *, tq, tk, n_sel, ns):
    i = pl.program_id(1)
    n_all = selk_ref.shape[0]
    nc = kc_ref.shape[1]
    blocks_per_tile = tk // SEL_BLOCK

    @pl.when(i == 0)
    def _():
        for src, k_ref, vt_ref in ((sel_ref, selk_ref, selvt_ref), (win_ref, wink_ref, winvt_ref)):
            for j in range(n_all):
                tile = src[0, j * tk:(j + 1) * tk, :]
                k_ref[j] = tile[:, :NSA_KV_WIDTH].astype(BF16)
                vt_ref[j] = tile[:, NSA_KV_WIDTH:].T.astype(BF16)
        kc = kc_ref[0]
        pad = jnp.zeros((LANES - nc, NSA_KV_WIDTH), F32)
        kck_ref[...] = jnp.concatenate([kc[:, :NSA_KV_WIDTH], pad], axis=0).astype(BF16)
        kcvt_ref[...] = jnp.concatenate([kc[:, NSA_KV_WIDTH:], pad], axis=0).T.astype(BF16)
        sel8_ref[...] = jnp.zeros_like(sel8_ref)

    t0 = i * tq
    feat = lax.broadcasted_iota(jnp.int32, (LANES, 1), 0)
    tlane = lax.broadcasted_iota(jnp.int32, (1, tq), 1)
    tvec = t0 + jnp.concatenate([tlane] * 4, axis=1)
    tvec2 = t0 + jnp.concatenate([tlane] * 2, axis=1)

    def stack_t(q):
        q_feat = q.T * SCALE
        parts = [jnp.where(_div(feat, HEAD_DIM) == g, q_feat[j * LANES:(j + 1) * LANES], 0.0)
                 for g in (0, 1) for j in (0, 1)]
        return jnp.concatenate(parts, axis=1).astype(BF16)

    qn_t, qr_t = stack_t(qn_ref[0]), stack_t(qr_ref[0])

    c = lax.broadcasted_iota(jnp.int32, (LANES, 1), 0)
    cvalid = (((c + 1) * CMP_BLOCK - 1) <= tvec) & (c < nc)
    s = jnp.where(cvalid, _dot(kck_ref[...], qn_t), NEG)
    e = jnp.exp(s - jnp.max(s, axis=0, keepdims=True))
    any_valid = jnp.max(cvalid.astype(F32), axis=0, keepdims=True)
    p = e / jnp.sum(e, axis=0, keepdims=True) * any_valid
    o_cmp = _dot(kcvt_ref[...], p.astype(BF16))

    imp_ref[0] = p[:, 0:tq] + p[:, tq:2 * tq]
    imp_ref[1] = p[:, 2 * tq:3 * tq] + p[:, 3 * tq:4 * tq]
    nb = sel8_ref.shape[0] * blocks_per_tile
    pair = jnp.concatenate([imp_ref[g, pl.ds(0, nb, stride=2), :] + imp_ref[g, pl.ds(1, nb, stride=2), :]
                            for g in (0, 1)], axis=1)
    sel = _rank_select(_block_scores(pair, tvec2), n_sel)
    for j in range(n_all):
        sel8_ref[j, 0:blocks_per_tile, :] = sel[j * blocks_per_tile:(j + 1) * blocks_per_tile, :]

    hi = (t0 + tq - 1) // tk + 1
    krow = lax.broadcasted_iota(jnp.int32, (tk, 1), 0)

    tq_vec = t0 + tlane

    def sel_mask(j, kpos):
        s8 = sel8_ref[j]
        chosen = s8[0:1, :]
        for bi in range(1, blocks_per_tile):
            chosen = jnp.where(krow >= bi * SEL_BLOCK, s8[bi:bi + 1, :], chosen)
        causal = kpos <= tq_vec
        return [(chosen[:, g * tq:(g + 1) * tq] > 0.5) & causal for g in (0, 1)]

    def win_mask(j, kpos):
        inside = (kpos <= tq_vec) & (kpos > tq_vec - WINDOW)
        return [inside, inside]

    qr_ts = [qr_t[:, ci * tq:(ci + 1) * tq] for ci in range(4)]
    o_sel = _flash_t(qr_ts, selk_ref, selvt_ref, 0, hi, tk, sel_mask)
    lo = jnp.maximum(t0 - WINDOW + 1, 0) // tk
    o_win = _flash_t(qr_ts, wink_ref, winvt_ref, lo, hi, tk, win_mask)

    gate_t = gate_ref[0].T
    parts = []
    for j in (0, 1):
        for g in (0, 1):
            ci = 2 * g + j
            o_c = o_cmp[g * HEAD_DIM:(g + 1) * HEAD_DIM, ci * tq:(ci + 1) * tq]
            y = 0.0
            for br, o in enumerate((o_c, o_sel[ci], o_win[ci])):
                r0 = br * BRANCH_WIDTH + j * LANES + g * HEAD_DIM
                y = y + gate_t[r0:r0 + HEAD_DIM] * o
            parts.append(y)
    o_ref[0] = jnp.concatenate(parts, axis=0).T


def _nsa_prompt(qn, qr, gate, kc, sel, win):
    b, t, _ = qn.shape
    tq, tk = min(ATT_TQ, t), min(NSA_TK, t)
    assert tq == LANES and tk // SEL_BLOCK <= SUBLANES
    assert tk % SEL_BLOCK == 0
    ns = t // SEL_BLOCK
    n_sel = min(N_SELECT, ns)
    n_tiles = t // tk
    qspec = lambda w: pl.BlockSpec((1, tq, w), lambda bi, i: (bi, i, 0))
    seq = lambda n, w: pl.BlockSpec((1, n, w), lambda bi, i: (bi, 0, 0))
    tiles = lambda r, c: pltpu.VMEM((n_tiles, r, c), BF16)
    return pl.pallas_call(
        functools.partial(_nsa_prompt_kernel, tq=tq, tk=tk, n_sel=n_sel, ns=ns),
        grid=(b, t // tq),
        in_specs=[qspec(256), qspec(256), qspec(768), seq(kc.shape[1], 256), seq(t, 256), seq(t, 256)],
        out_specs=qspec(256),
        out_shape=jax.ShapeDtypeStruct((b, t, BRANCH_WIDTH), F32),
        scratch_shapes=[tiles(tk, NSA_KV_WIDTH), tiles(NSA_KV_WIDTH, tk), tiles(tk, NSA_KV_WIDTH),
                        tiles(NSA_KV_WIDTH, tk),
                        pltpu.VMEM((LANES, NSA_KV_WIDTH), BF16), pltpu.VMEM((NSA_KV_WIDTH, LANES), BF16),
                        pltpu.VMEM((2, LANES, tq), F32), pltpu.VMEM((n_tiles, SUBLANES, 2 * tq), F32)],
        compiler_params=_params("parallel", "arbitrary"),
        name="nsa_prompt",
    )(qn, qr, gate, kc, sel, win)


def _compressed_attend(qn, kc, tvec):
    nc = kc.shape[0]
    pad = jnp.zeros((LANES - nc, NSA_KV_WIDTH), F32)
    kck = jnp.concatenate([kc[:, :NSA_KV_WIDTH], pad], axis=0).astype(BF16)
    kcv = jnp.concatenate([kc[:, NSA_KV_WIDTH:], pad], axis=0).astype(BF16)
    lane = lax.broadcasted_iota(jnp.int32, (1, LANES), 1)
    valid = (((lane + 1) * CMP_BLOCK - 1) <= tvec) & (lane < nc)
    s = jnp.where(valid, _dot_nt(qn, kck), NEG)
    e = jnp.exp(s - jnp.max(s, axis=-1, keepdims=True))
    any_valid = jnp.max(valid.astype(F32), axis=-1, keepdims=True)
    p = e / jnp.sum(e, axis=-1, keepdims=True) * any_valid
    return p, _dot(p.astype(BF16), kcv)


def _softmax_scores(raw, masks, vts):
    scores = [jnp.where(msk, s, NEG) for s, msk in zip(raw, masks)]
    m = functools.reduce(jnp.maximum, [jnp.max(s, axis=-1, keepdims=True) for s in scores])
    es = [jnp.exp(s - m) for s in scores]
    l = sum(jnp.sum(e, axis=-1, keepdims=True) for e in es)
    o = sum(_dot_nt(e.astype(BF16), vt) for e, vt in zip(es, vts))
    return o / l


def _nsa_sample_kernel(pt_ref, qn_ref, qr_ref, gate_ref, kc_ref, seln_ref, win_ref, winn_ref, e_ref, *rest,
                       n_pages, nd, past, n_sel, nb):
    pages, (o_ref, wino_ref, newt_ref, impt_ref) = rest[:nb * n_pages], rest[nb * n_pages + 1:]

    @pl.when(pl.program_id(0) == 0)
    def _():
        newt_ref[...] = jnp.zeros_like(newt_ref)

    rows = 4 * nd
    row = lax.broadcasted_iota(jnp.int32, (rows, 1), 0)
    lane = lax.broadcasted_iota(jnp.int32, (1, LANES), 1)
    group_mask = (_div(lane, HEAD_DIM) == _div(row, 2 * nd)).astype(F32)
    tau = _mod(row, nd)
    page = pages[0].shape[1]
    wb = win_ref.shape[2]
    assert page == LANES

    k_half = lambda x: x[:NSA_KV_WIDTH].astype(BF16)
    v_half = lambda x: x[NSA_KV_WIDTH:].astype(BF16)

    ps, o_cmps, o_wins, sel_raw, sel_vts = [], [], [], [], []
    for s in range(nb):
        newt_ref[s, 0, 0:nd, :] = seln_ref[s]
        newt_ref[s, 1, 0:nd, :] = winn_ref[s]
        qn = (qn_ref[s] * SCALE * group_mask).astype(BF16)
        qr = (qr_ref[s] * SCALE * group_mask).astype(BF16)
        p, o_cmp = _compressed_attend(qn, kc_ref[s], past + tau)
        ps.append(p)
        o_cmps.append(o_cmp)

        new_win_t = newt_ref[s, 1].T
        win_tiles = [win_ref[s, :, wi * page:(wi + 1) * page] for wi in range(wb // page)] + [new_win_t]
        masks = [(wi * page + lane > tau + (wb - WINDOW)) & (wi * page + lane >= wb - past)
                 for wi in range(wb // page)] + [lane <= tau]
        o_wins.append(_softmax_scores([_dot(qr, k_half(wt)) for wt in win_tiles], masks,
                                      [v_half(wt) for wt in win_tiles]))
        rolled = pltpu.roll(win_ref[s], wb - nd, 1)
        wino_ref[s, :, 0:wb - page] = rolled[:, 0:wb - page]
        wino_ref[s, :, wb - page:wb] = jnp.where(lane >= page - nd, pltpu.roll(new_win_t, page - nd, 1),
                                                 rolled[:, wb - page:wb])

        sel_tiles = [pages[s * n_pages + pi] for pi in range(n_pages)] + [newt_ref[s, 0].T]
        sel_raw.append([_dot(qr, k_half(st)) for st in sel_tiles])
        sel_vts.append(sel_tiles)

    p_all = jnp.concatenate(ps, axis=0)
    imp = p_all + pltpu.roll(p_all, nd, 0)
    impt_ref[...] = jnp.concatenate([imp, jnp.zeros((LANES - nb * rows, LANES), F32)], axis=0).T
    nbk = -(-((past + nd - 1) // SEL_BLOCK + 1) // SUBLANES) * SUBLANES
    pair = impt_ref[pl.ds(0, nbk, stride=2), :] + impt_ref[pl.ds(1, nbk, stride=2), :]
    sel_t = _rank_select(_block_scores(pair, past + _mod(lane, nd)), n_sel)
    sel_rn = jnp.concatenate([sel_t, jnp.zeros((LANES - nbk, LANES), F32)], axis=0).T[:nb * rows]
    row_all = lax.broadcasted_iota(jnp.int32, (nb * rows, 1), 0)
    sel4_all = jnp.where(_mod(_div(row_all, nd), 2) == 1, sel_rn,
                         pltpu.roll(sel_rn, nb * rows - nd, 0)).astype(BF16)

    for s in range(nb):
        sel4 = sel4_all[s * rows:(s + 1) * rows]
        masks = [_dot(sel4, e_ref[pi]) > 0.5 for pi in range(n_pages)]
        masks.append((_dot(sel4, e_ref[n_pages]) > 0.5) & (lane <= tau))
        o_sel = _softmax_scores(sel_raw[s], masks, [v_half(st) for st in sel_vts[s]])
        y = (gate_ref[s, 0] * o_cmps[s] + gate_ref[s, 1] * o_sel + gate_ref[s, 2] * o_wins[s]) * group_mask
        o_ref[s] = y[0:2 * nd] + y[2 * nd:4 * nd]


def _expand_matrix(n_tiles, tk):
    n = np.arange(LANES)[None, :, None]
    key = (np.arange(n_tiles)[:, None, None] * tk + np.arange(tk)[None, None, :])
    return jnp.asarray(n == key // SEL_BLOCK, BF16)


def _nsa_sample(page_table, qn16, qr16, gate16, kc, sel_new, win_state, win_new, cache_sel, layer, past, win_acc):
    ns, n_pages = page_table.shape
    nd = sel_new.shape[1]
    assert nd == 4
    width, page = cache_sel.shape[2:]
    wb = win_state.shape[3]
    assert wb % page == 0 and page == LANES
    n_sel = min(N_SELECT, -(-(past + nd) // SEL_BLOCK))
    e = _expand_matrix(n_pages + 1, page)
    nb = math.gcd(ns, SAMPLE_SEQS_PER_STEP)
    per_seq = lambda *s: pl.BlockSpec((nb,) + s, lambda b, pt, n=len(s): (b,) + (0,) * n)
    win_spec = pl.BlockSpec((None, nb, 256, wb), lambda b, pt: (layer, b, 0, 0))
    n_in = 9 + nb * n_pages
    grid_spec = pltpu.PrefetchScalarGridSpec(
        num_scalar_prefetch=1,
        grid=(ns // nb,),
        in_specs=[per_seq(16, LANES), per_seq(16, LANES), per_seq(3, 16, LANES), per_seq(kc.shape[1], 256),
                  per_seq(nd, 256), win_spec, per_seq(nd, 256),
                  pl.BlockSpec(e.shape, lambda b, pt: (0, 0, 0))] + _page_specs(cache_sel, layer, n_pages, nb)
        + [pl.BlockSpec(memory_space=pl.ANY)],
        out_specs=[per_seq(2 * nd, LANES), win_spec],
        scratch_shapes=[pltpu.VMEM((nb, 2, page, 256), F32), pltpu.VMEM((LANES, LANES), F32)],
    )
    return pl.pallas_call(
        functools.partial(_nsa_sample_kernel, n_pages=n_pages, nd=nd, past=past, n_sel=n_sel, nb=nb),
        grid_spec=grid_spec,
        out_shape=[jax.ShapeDtypeStruct((ns, 2 * nd, LANES), F32), jax.ShapeDtypeStruct(win_acc.shape, F32)],
        input_output_aliases={n_in: 1},
        compiler_params=_params("arbitrary"),
        name="nsa_sample",
    )(page_table, qn16, qr16, gate16, kc, sel_new, win_state, win_new, e, *([cache_sel] * (nb * n_pages)), win_acc)


_SC_HALO, _CC_HALO = 8, 32


def _conv_kernel(u_ref, scb_ref, a_ref, uprev_ref, aprev_ref, stsc_ref, stcc_ref,
                 scw_ref, ccw_ref, ccb_ref, ccg_ref, ccbeta_ref,
                 yc_ref, yd_ref, nsc_ref, ncc_ref, fu_ref, fa_ref, *, tc):
    c = pl.program_id(1)

    @pl.when(c == 0)
    def _():
        fu_ref[:, _SC_HALO - (SC_CONV_LEN - 1):_SC_HALO, :] = stsc_ref[...]
        fa_ref[:, _CC_HALO - (CC_CONV_LEN - 1):_CC_HALO, :] = stcc_ref[...]

    @pl.when(c > 0)
    def _():
        fu_ref[:, 0:_SC_HALO, :] = uprev_ref[...]
        fa_ref[:, 0:_CC_HALO, :] = aprev_ref[...]

    fu_ref[:, _SC_HALO:, :] = u_ref[...]
    fa_ref[:, _CC_HALO:, :] = a_ref[...]
    sub = min(CONV_SUB, tc)
    for s in range(tc // sub):
        r0 = s * sub
        acc = 0.0
        for i in range(SC_CONV_LEN):
            o = r0 + i + _SC_HALO - (SC_CONV_LEN - 1)
            acc = acc + scw_ref[i:i + 1, :] * fu_ref[:, o:o + sub, :]
        yc_ref[:, r0:r0 + sub, :] = scb_ref[:, r0:r0 + sub, :] * acc
        acc = 0.0
        for i in range(CC_CONV_LEN):
            o = r0 + i + _CC_HALO - (CC_CONV_LEN - 1)
            acc = acc + ccw_ref[i:i + 1, :] * fa_ref[:, o:o + sub, :]
        z = acc + ccb_ref[...]
        mu = jnp.mean(z, axis=-1, keepdims=True)
        var = jnp.mean(jnp.square(z - mu), axis=-1, keepdims=True)
        ln = (z - mu) * lax.rsqrt(var + NORM_EPS) * ccg_ref[...] + ccbeta_ref[...]
        yd_ref[:, r0:r0 + sub, :] = ln * jax.nn.sigmoid(ln)
    nsc_ref[...] = fu_ref[:, tc + _SC_HALO - (SC_CONV_LEN - 1):tc + _SC_HALO, :]
    ncc_ref[...] = fa_ref[:, tc + _CC_HALO - (CC_CONV_LEN - 1):tc + _CC_HALO, :]


def _conv(u, scb, a, st_sc, st_cc, scw, ccw, ccb, ccg, ccbeta, nb):
    b, t, w = u.shape
    tc = min(CONV_TC, t)
    chunked = t > tc
    if chunked:
        uprev, aprev = u, a
        prev_u = pl.BlockSpec((nb, _SC_HALO, w), lambda bi, c: (bi, jnp.maximum(c * (tc // _SC_HALO) - 1, 0), 0))
        prev_a = pl.BlockSpec((nb, _CC_HALO, w), lambda bi, c: (bi, jnp.maximum(c * (tc // _CC_HALO) - 1, 0), 0))
    else:
        uprev, aprev = jnp.zeros((b, _SC_HALO, w), F32), jnp.zeros((b, _CC_HALO, w), F32)
        prev_u = pl.BlockSpec((nb, _SC_HALO, w), lambda bi, c: (bi, 0, 0))
        prev_a = pl.BlockSpec((nb, _CC_HALO, w), lambda bi, c: (bi, 0, 0))
    cur = pl.BlockSpec((nb, tc, w), lambda bi, c: (bi, c, 0))
    state = lambda n: pl.BlockSpec((nb, n, w), lambda bi, c: (bi, 0, 0))
    wspec = lambda arr: pl.BlockSpec(arr.shape, lambda bi, c: (0, 0))
    return pl.pallas_call(
        functools.partial(_conv_kernel, tc=tc),
        grid=(b // nb, t // tc),
        in_specs=[cur, cur, cur, prev_u, prev_a, state(SC_CONV_LEN - 1), state(CC_CONV_LEN - 1),
                  wspec(scw), wspec(ccw), wspec(ccb), wspec(ccg), wspec(ccbeta)],
        out_specs=[cur, cur, state(SC_CONV_LEN - 1), state(CC_CONV_LEN - 1)],
        out_shape=[jax.ShapeDtypeStruct((b, t, w), F32), jax.ShapeDtypeStruct((b, t, w), F32),
                   jax.ShapeDtypeStruct((b, SC_CONV_LEN - 1, w), F32),
                   jax.ShapeDtypeStruct((b, CC_CONV_LEN - 1, w), F32)],
        scratch_shapes=[pltpu.VMEM((nb, _SC_HALO + tc, w), F32), pltpu.VMEM((nb, _CC_HALO + tc, w), F32)],
        compiler_params=_params("parallel", "arbitrary"),
        name="conv",
    )(u, scb, a, uprev, aprev, st_sc, st_cc, scw, ccw, ccb, ccg, ccbeta)


def _merge_kernel(x_ref, ya_ref, yb_ref, yc_ref, yd_ref, g_ref, wg_ref, bg_ref, wb_ref, wo_ref, o_ref):
    x = x_ref[...]
    d = x.shape[-1]
    h = _rms(x, g_ref[...]).astype(BF16)
    merged = 0.0
    for n, y_ref in enumerate((ya_ref, yb_ref, yc_ref, yd_ref)):
        gate = jax.nn.sigmoid(_dot(h, wg_ref[:, n * d:(n + 1) * d]) + bg_ref[:, n * d:(n + 1) * d])
        merged = merged + gate * _dot(y_ref[...].astype(BF16), wb_ref[n])
    o_ref[...] = x + _dot(merged.astype(BF16), wo_ref[...])


def _merge(x, ya, yb, yc, yd, g, wg, bg, wb, wo, layer):
    n, d = x.shape
    tm = min(TOKEN_TILE, n)
    tok = lambda w: pl.BlockSpec((tm, w), lambda i: (i, 0))
    return pl.pallas_call(
        _merge_kernel,
        grid=(n // tm,),
        in_specs=[tok(d)] + [tok(BRANCH_WIDTH)] * 4 + [_layer(a, layer) for a in (g, wg, bg, wb, wo)],
        out_specs=tok(d),
        out_shape=jax.ShapeDtypeStruct((n, d), F32),
        compiler_params=_params("parallel"),
        name="merge",
    )(x, ya, yb, yc, yd, g, wg, bg, wb, wo)


def _regroup_w_in(w_in):
    splits = (BRANCH_WIDTH,) * 4 + (NSA_KV_WIDTH,) * 6 + (12,) + (BRANCH_WIDTH,) * 5
    offs = np.concatenate([[0], np.cumsum(splits)])
    o_nq, o_ck, o_ng, o_scb = offs[3], offs[4], offs[10], offs[11]
    heads_jg = [g * 2 + j for j in (0, 1) for g in (0, 1)]
    jg_heads = np.concatenate([np.arange(h * HEAD_DIM, (h + 1) * HEAD_DIM) for h in heads_jg])
    nq = [w_in[..., o_nq + h * HEAD_DIM: o_nq + (h + 1) * HEAD_DIM] for h in heads_jg]
    gate_cols = np.array([o_ng + g * 6 + j * 3 + br for br in range(3) for j in (0, 1) for g in (0, 1)])
    gates = jnp.repeat(w_in[..., gate_cols], HEAD_DIM, axis=-1)
    out = jnp.concatenate([w_in[..., :o_nq]] + nq + [w_in[..., o_ck:o_ng], gates, w_in[..., o_scb:]], axis=-1)
    assert out.shape[-1] == _P_END
    return out.astype(BF16), jg_heads


def _rope_tables(pos):
    half = HEAD_DIM // 2
    inv_freq = jnp.exp(-math.log(ROPE_THETA) * jnp.arange(half, dtype=F32) / half)
    ang = pos.astype(F32)[:, None] * inv_freq[None, :]
    cos, sin = jnp.cos(ang), jnp.sin(ang)
    reps = BRANCH_WIDTH // HEAD_DIM
    return jnp.tile(jnp.concatenate([cos, cos], -1), (1, reps)), jnp.tile(jnp.concatenate([-sin, sin], -1), (1, reps))


def _rows16(a, nd):
    ns = a.shape[0]
    a = a.reshape(ns, nd, 2, LANES).transpose(0, 2, 1, 3)
    return jnp.broadcast_to(a[:, None], (ns, 2, 2, nd, LANES)).reshape(ns, 4 * nd, LANES)


def _feature_major(cache):
    l, p, rows = cache.shape[:3]
    return cache.transpose(0, 1, 3, 4, 5, 2).reshape(l, p, -1, rows)


def kernel(x_prompt, x_sample, cache_sb_kv, cache_cmp_kv, cache_sel_kv, state_win_kv, state_conv_sc,
           state_conv_cc, page_table, norm_ffn1, w_ffn1_gu, w_ffn1_down, norm_mix, w_in, nsa_q_gain,
           nsa_k_gain, nsa_cmp_pe, nsa_cmp_w, sc_conv_w, cc_conv_w, cc_conv_b, cc_norm_g, cc_norm_b,
           w_branch, w_gate, b_gate, w_out, norm_ffn2, w_ffn2_gu, w_ffn2_down):
    b, t, d = x_prompt.shape
    ns, nd, _ = x_sample.shape
    depth, n_phys, page = cache_sb_kv.shape[:3]
    n_pages = page_table.shape[1]
    past = n_pages * page
    assert w_ffn1_down.shape[1] % FFN_CHUNK == 0
    wb = state_win_kv.shape[2]

    wgu1, wgu2 = w_ffn1_gu.astype(BF16), w_ffn2_gu.astype(BF16)
    wd1, wd2 = w_ffn1_down.astype(BF16), w_ffn2_down.astype(BF16)
    w_in_r, jg_heads = _regroup_w_in(w_in)
    w_gate_b = w_gate.astype(BF16)
    w_branch_b = w_branch.at[:, 1].set(w_branch[:, 1][:, jg_heads, :]).astype(BF16)
    w_out_b = w_out.astype(BF16)
    eye2 = jnp.eye(2, dtype=F32)
    w_cmp = jnp.einsum("Lklde,gG->LklgdGe", nsa_cmp_w, eye2).reshape(depth, 2, CMP_BLOCK, 128, 128)
    w_cmp = w_cmp.astype(BF16)
    pe_rows = jnp.tile(nsa_cmp_pe, (1, 1, 1, 2))
    row2 = lambda a: a.reshape(depth, 1, -1)
    n1, nm, n2, bg = row2(norm_ffn1), row2(norm_mix), row2(norm_ffn2), row2(b_gate)
    qg = jnp.tile(row2(nsa_q_gain), (1, 1, 4))
    kg = jnp.tile(row2(nsa_k_gain), (1, 1, 2))
    ccb, ccg, ccbeta = row2(cc_conv_b), row2(cc_norm_g), row2(cc_norm_b)
    hd = np.arange(BRANCH_WIDTH) // HEAD_DIM
    bd = jnp.asarray(hd[:, None] == hd[None, :], BF16)
    bd1 = bd[:LANES, :LANES]
    cos_p, sin_p = _rope_tables(jnp.arange(t, dtype=jnp.int32))
    cos_s, sin_s = _rope_tables(past + jnp.arange(ns * nd, dtype=jnp.int32) % nd)

    cache_sb = _feature_major(cache_sb_kv)
    cache_cmp = _feature_major(cache_cmp_kv)
    cache_sel = _feature_major(cache_sel_kv)
    win_state = _feature_major(state_win_kv)

    xp = x_prompt.reshape(b * t, d)
    xs = x_sample.reshape(ns * nd, d)
    zeros_sc = jnp.zeros((b, SC_CONV_LEN - 1, BRANCH_WIDTH), F32)
    zeros_cc = jnp.zeros((b, CC_CONV_LEN - 1, BRANCH_WIDTH), F32)
    tw = min(WINDOW, t)
    assert t % min(TOKEN_TILE, b * t) == 0 and tw in (t, min(TOKEN_TILE, b * t))
    stacked_p = tuple(jnp.zeros((depth, b, w_, t_), F32) for w_, t_ in ((512, t), (256, t), (256, t), (256, tw)))
    win_acc = jnp.zeros((depth, ns, 256, wb), F32)
    outs_p = [[] for _ in range(2)]
    outs_s = [[] for _ in range(5)]
    for l in range(depth):
        xp = _ffn(xp, n1, wgu1, wd1, l)
        xs = _ffn(xs, n1, wgu1, wd1, l)

        (sbq, sbkv, qn, qr, cmp_, sel, win, gate, scb, scu, cca), stacked_p = _proj(
            xp, nm, w_in_r, cos_p, sin_p, qg, kg, bd, l, stacked_p)
        seq = lambda a: a.reshape(b, t, a.shape[-1])
        ya = _sb_prompt(seq(sbq), seq(sbkv))
        kc = _compress_prompt(seq(cmp_), pe_rows, w_cmp, kg, bd1, l)
        yb = _nsa_prompt(seq(qn), seq(qr), seq(gate), kc, seq(sel), seq(win))
        yc, yd, nsc, ncc = _conv(seq(scu), seq(scb), seq(cca), zeros_sc, zeros_cc, sc_conv_w[l], cc_conv_w[l],
                                 ccb[l], ccg[l], ccbeta[l], nb=1)
        flat = lambda a: a.reshape(b * t, a.shape[-1])
        xp = _merge(xp, flat(ya), flat(yb), flat(yc), flat(yd), nm, w_gate_b, bg, w_branch_b, w_out_b, l)
        for lst, a in zip(outs_p, (nsc, ncc)):
            lst.append(a)

        (sbq, sbkv, qn, qr, cmp_, sel, win, gate, scb, scu, cca), _ = _proj(
            xs, nm, w_in_r, cos_s, sin_s, qg, kg, bd, l)
        seq = lambda a: a.reshape(ns, nd, a.shape[-1])
        q16 = jnp.tile(seq(sbq), (1, 4, 1))
        ya = _sb_sample(page_table, q16, seq(sbkv), cache_sb, l)
        kc = _compress_sample(page_table, cache_cmp, l, pe_rows, w_cmp, kg, bd1)
        gate16 = jnp.stack([_rows16(seq(gate)[:, :, br * 256:(br + 1) * 256], nd) for br in range(3)], axis=1)
        y8, win_acc = _nsa_sample(page_table, _rows16(seq(qn), nd), _rows16(seq(qr), nd), gate16, kc,
                                  seq(sel), win_state, seq(win), cache_sel, l, past, win_acc)
        yb = y8.reshape(ns, 2, nd, LANES).transpose(0, 2, 1, 3).reshape(ns, nd, BRANCH_WIDTH)
        yc, yd, nsc, ncc = _conv(seq(scu), seq(scb), seq(cca), state_conv_sc[l], state_conv_cc[l],
                                 sc_conv_w[l], cc_conv_w[l], ccb[l], ccg[l], ccbeta[l], nb=min(8, ns))
        flat = lambda a: a.reshape(ns * nd, a.shape[-1])
        xs = _merge(xs, flat(ya), flat(yb), flat(yc), flat(yd), nm, w_gate_b, bg, w_branch_b, w_out_b, l)
        for lst, a in zip(outs_s, (seq(sbkv), seq(cmp_), seq(sel), nsc, ncc)):
            lst.append(a)

        xp = _ffn(xp, n2, wgu2, wd2, l)
        xs = _ffn(xs, n2, wgu2, wd2, l)

    def kv5(a, heads):
        a = jnp.stack(a)
        return a.reshape(a.shape[:3] + (2, heads, HEAD_DIM))

    def rows_last(a, heads):
        return a.reshape(a.shape[:2] + (2, heads, HEAD_DIM, a.shape[3])).transpose(0, 1, 5, 2, 3, 4)

    sb_heads, nsa_kv_heads = BRANCH_WIDTH // HEAD_DIM, NSA_KV_WIDTH // HEAD_DIM
    return (xp.reshape(b, t, d), xs.reshape(ns, nd, d),
            rows_last(stacked_p[0], sb_heads), kv5(outs_s[0], sb_heads),
            rows_last(stacked_p[1], nsa_kv_heads), kv5(outs_s[1], nsa_kv_heads),
            rows_last(stacked_p[2], nsa_kv_heads), kv5(outs_s[2], nsa_kv_heads),
            rows_last(stacked_p[3], nsa_kv_heads), rows_last(win_acc, nsa_kv_heads),
            jnp.stack(outs_p[0]), jnp.stack(outs_s[3]), jnp.stack(outs_p[1]), jnp.stack(outs_s[4]))
```

```python
import functools
import math

import numpy as np
import jax
import jax.numpy as jnp
from jax import lax
from jax.experimental import pallas as pl
from jax.experimental.pallas import tpu as pltpu

F32, BF16 = jnp.float32, jnp.bfloat16

HEAD_DIM = 64
BRANCH_WIDTH = 256
N_BRANCH = 4
NSA_KV_WIDTH = 128
CMP_BLOCK = 32
SEL_BLOCK = 64
N_SELECT = 16
WINDOW = 512
SC_CONV_LEN = 3
CC_CONV_LEN = 31
ROPE_THETA = 10000.0
NORM_EPS = 1e-6
NEG = -1e30
SCALE = HEAD_DIM ** -0.5

LANES = 128
SUBLANES = 8
VMEM_LIMIT_BYTES = 56 * 1024 * 1024
TOKEN_TILE = 512
FFN_CHUNK = 256
ATT_TQ = 128
SB_TQ = 256
SB_TK = 256
NSA_TK = 512
SAMPLE_SEQS_PER_STEP = 4
CMP_PITCH = CMP_BLOCK + 1
CONV_TC = 256
CONV_SUB = 64

_NT = (((1,), (1,)), ((), ()))


def _params(*sem):
    return pltpu.CompilerParams(dimension_semantics=sem, vmem_limit_bytes=VMEM_LIMIT_BYTES)


def _full(shape):
    n = len(shape)
    return pl.BlockSpec(shape, lambda *_, n=n: (0,) * n)


def _layer(arr, layer):
    n = arr.ndim - 1
    return pl.BlockSpec((None,) + arr.shape[1:], lambda *_, n=n: (layer,) + (0,) * n)


def _dot(a, b):
    return jnp.dot(a, b, preferred_element_type=F32)


def _dot_nt(a, b):
    return lax.dot_general(a, b, _NT, preferred_element_type=F32)


def _div(x, n):
    assert n & (n - 1) == 0
    return x >> (n.bit_length() - 1)


def _mod(x, n):
    assert n & (n - 1) == 0
    return x & (n - 1)


def _rms(x, g):
    return x * lax.rsqrt(jnp.mean(x * x, axis=-1, keepdims=True) + NORM_EPS) * g


def _split(x):
    hi = x.astype(BF16)
    return hi, (x - hi.astype(F32)).astype(BF16)


def _split_dot(x, m):
    hi, lo = _split(x)
    return _dot(hi, m) + _dot(lo, m)


def _head_rms(x, bd, g):
    ms = _split_dot(x * x, bd) * (1.0 / HEAD_DIM)
    return x * lax.rsqrt(ms + NORM_EPS) * g


def _rope(x, cos, sin_signed):
    w = x.shape[-1]
    lane = lax.broadcasted_iota(jnp.int32, x.shape, 1)
    first_half = (lane & (HEAD_DIM - 1)) < HEAD_DIM // 2
    rot = jnp.where(first_half, pltpu.roll(x, w - HEAD_DIM // 2, 1), pltpu.roll(x, HEAD_DIM // 2, 1))
    return x * cos + rot * sin_signed


def _ffn_kernel(x_ref, g_ref, wgu_ref, wd_ref, o_ref, h_ref, acc_ref):
    f = wd_ref.shape[0]
    h_ref[...] = _rms(x_ref[...], g_ref[...]).astype(BF16)
    for c in range(f // FFN_CHUNK):
        cols = slice(c * FFN_CHUNK, (c + 1) * FFN_CHUNK)
        h = h_ref[...]
        g = _dot(h, wgu_ref[:, cols])
        u = _dot(h, wgu_ref[:, f + c * FFN_CHUNK: f + (c + 1) * FFN_CHUNK])
        a = (g * jax.nn.sigmoid(g) * u).astype(BF16)
        down = _dot(a, wd_ref[cols, :])
        if c == 0:
            acc_ref[...] = down
        else:
            acc_ref[...] += down
    o_ref[...] = x_ref[...] + 0.5 * acc_ref[...]


def _ffn(x, g, wgu, wd, layer):
    n, d = x.shape
    tm = min(TOKEN_TILE, n)
    return pl.pallas_call(
        _ffn_kernel,
        grid=(n // tm,),
        in_specs=[pl.BlockSpec((tm, d), lambda i: (i, 0)), _layer(g, layer), _layer(wgu, layer),
                  _layer(wd, layer)],
        out_specs=pl.BlockSpec((tm, d), lambda i: (i, 0)),
        out_shape=jax.ShapeDtypeStruct((n, d), F32),
        scratch_shapes=[pltpu.VMEM((tm, d), BF16), pltpu.VMEM((tm, d), F32)],
        compiler_params=_params("parallel"),
        name="ffn",
    )(x, g, wgu, wd)


_P_SBQ, _P_SBKV, _P_NQ, _P_CMP, _P_SEL, _P_WIN, _P_GATE = 0, 256, 768, 1024, 1280, 1536, 1792
_P_SCB, _P_SCC, _P_SCH, _P_CCA, _P_CCG, _P_END = 2560, 2816, 3072, 3328, 3584, 3840


def _proj_kernel(x_ref, g_ref, w_ref, cos_ref, sin_ref, qg_ref, kg_ref, bd_ref, *rest, n_stacked):
    (sbq_ref, sbkv_ref, qn_ref, qr_ref, cmp_ref, sel_ref, win_ref, gate_ref,
     scb_ref, scu_ref, cca_ref) = rest[n_stacked:n_stacked + 11]
    h = _rms(x_ref[...], g_ref[...]).astype(BF16)

    def mm(a, b):
        return _dot(h, w_ref[:, a:b])

    cos, sin, bd = cos_ref[...], sin_ref[...], bd_ref[...]
    sbq_ref[...] = mm(_P_SBQ, _P_SBKV)
    sbkv_ref[...] = mm(_P_SBKV, _P_NQ)
    qn = _head_rms(mm(_P_NQ, _P_CMP), bd, qg_ref[...])
    qn_ref[...] = qn
    qr_ref[...] = _rope(qn, cos, sin)
    cmp_ref[...] = mm(_P_CMP, _P_SEL)
    bd1, cos1, sin1 = bd[:LANES, :LANES], cos[:, :LANES], sin[:, :LANES]
    for a, ref in ((_P_SEL, sel_ref), (_P_WIN, win_ref)):
        kv = mm(a, a + 2 * NSA_KV_WIDTH)
        ref[:, :NSA_KV_WIDTH] = _rope(_head_rms(kv[:, :NSA_KV_WIDTH], bd1, kg_ref[...]), cos1, sin1)
        ref[:, NSA_KV_WIDTH:] = kv[:, NSA_KV_WIDTH:]
    gate_ref[...] = jax.nn.sigmoid(mm(_P_GATE, _P_SCB))
    scb_ref[...] = mm(_P_SCB, _P_SCC)
    scu_ref[...] = mm(_P_SCC, _P_SCH) * mm(_P_SCH, _P_CCA)
    cca_ref[...] = mm(_P_CCA, _P_CCG) * jax.nn.sigmoid(mm(_P_CCG, _P_END))
    for src, dst in zip((sbkv_ref, cmp_ref, sel_ref, win_ref), rest[n_stacked + 11:]):
        dst[0] = src[...].T


def _proj(x, g, w, cos, sin, qg, kg, bd, layer, stacked=()):
    n, d = x.shape
    tm = min(TOKEN_TILE, n)
    n_rope_tiles = cos.shape[0] // tm
    widths = (256, 512, 256, 256, 256, 256, 256, 768, 256, 256, 256)
    tok = lambda wd_: pl.BlockSpec((tm, wd_), lambda i: (i, 0))
    rope = pl.BlockSpec((tm, cos.shape[1]), lambda i: (i % n_rope_tiles, 0))
    out_specs = [tok(wd_) for wd_ in widths]
    out_shape = [jax.ShapeDtypeStruct((n, wd_), F32) for wd_ in widths]
    for buf in stacked:
        tiles_per_seq = n // buf.shape[1] // tm
        if buf.shape[3] == tiles_per_seq * tm:
            spec = pl.BlockSpec((None, 1, buf.shape[2], tm),
                                lambda i, tps=tiles_per_seq: (layer, i // tps, 0, i % tps))
        else:
            assert buf.shape[3] == tm
            spec = pl.BlockSpec((None, 1, buf.shape[2], tm), lambda i, tps=tiles_per_seq: (layer, i // tps, 0, 0))
        out_specs.append(spec)
        out_shape.append(jax.ShapeDtypeStruct(buf.shape, F32))
    n_in = 8
    outs = pl.pallas_call(
        functools.partial(_proj_kernel, n_stacked=len(stacked)),
        grid=(n // tm,),
        in_specs=[tok(d), _layer(g, layer), _layer(w, layer), rope, rope, _layer(qg, layer), _layer(kg, layer),
                  _full(bd.shape)] + [pl.BlockSpec(memory_space=pl.ANY)] * len(stacked),
        out_specs=out_specs,
        out_shape=out_shape,
        input_output_aliases={n_in + k: len(widths) + k for k in range(len(stacked))},
        compiler_params=_params("arbitrary"),
        name="proj",
    )(x, g, w, cos, sin, qg, kg, bd, *stacked)
    return outs[:len(widths)], outs[len(widths):]


def _sb_logs(z):
    log_beta = jnp.minimum(z, 0.0) - jnp.log(1.0 + jnp.exp(-jnp.abs(z)))
    return log_beta, log_beta - z


def _sb_tile_t(k, vts, q_ts, valid, carry, tri_t):
    logs = [_sb_logs(_dot(k, q)) for q in q_ts]
    keeps = [lk if valid is None else jnp.where(valid, lk, 0.0) for _, lk in logs]
    between = [_dot(tri_t, hi) + _dot(tri_t, lo) for hi, lo in [_split(lk) for lk in keeps]]
    ws = [jnp.exp(lb + bt + c) for (lb, _), bt, (c, _) in zip(logs, between, carry)]
    if valid is not None:
        ws = [jnp.where(valid, w, 0.0) for w in ws]
    pvs = [_dot(vt, w.astype(BF16)) for vt, w in zip(vts, ws)]
    return tuple((c + bt[0:1] + lk[0:1], acc + pv)
                 for (c, acc), bt, lk, pv in zip(carry, between, keeps, pvs))


def _sb_prompt_kernel(q_ref, kv_ref, tri_ref, o_ref, kb_ref, vtb_ref, *, tq, tk):
    i = pl.program_id(1)
    n_all = kb_ref.shape[0]

    @pl.when(i == 0)
    def _():
        for j in range(n_all):
            tile = kv_ref[0, j * tk:(j + 1) * tk, :]
            kb_ref[j] = tile[:, :BRANCH_WIDTH].astype(BF16)
            vtb_ref[j] = tile[:, BRANCH_WIDTH:].T.astype(BF16)

    t0 = i * tq
    heads = BRANCH_WIDTH // HEAD_DIM
    feat = lax.broadcasted_iota(jnp.int32, (BRANCH_WIDTH, 1), 0)
    q_feat = q_ref[0].T * SCALE
    q_h = [jnp.where(_div(feat, HEAD_DIM) == h, q_feat, 0.0).astype(BF16) for h in range(heads)]
    tvec = t0 + lax.broadcasted_iota(jnp.int32, (1, tq), 1)
    tri_t = tri_ref[...]
    j_diag = (t0 + tq - 1) // tk
    kpos = j_diag * tk + lax.broadcasted_iota(jnp.int32, (tk, 1), 0)

    def tile(j, carry, valid):
        vts = [vtb_ref[j, h * HEAD_DIM:(h + 1) * HEAD_DIM, :] for h in range(heads)]
        return _sb_tile_t(kb_ref[j], vts, q_h, valid, carry, tri_t)

    init = tuple((jnp.zeros((1, tq), F32), jnp.zeros((HEAD_DIM, tq), F32)) for _ in range(heads))
    carry = tile(j_diag, init, kpos < tvec)
    carry = lax.fori_loop(0, j_diag, lambda it, carry: tile(j_diag - 1 - it, carry, None), carry)
    o_ref[0] = jnp.concatenate([acc for _, acc in carry], axis=0).T


def _tri(n, upper):
    r = np.arange(n)
    m = r[:, None] < r[None, :] if upper else r[:, None] > r[None, :]
    return jnp.asarray(m, BF16)


def _sb_prompt(q, kv):
    b, t, _ = q.shape
    tq, tk = min(SB_TQ, t), min(SB_TK, t)
    assert tk % tq == 0
    return pl.pallas_call(
        functools.partial(_sb_prompt_kernel, tq=tq, tk=tk),
        grid=(b, t // tq),
        in_specs=[pl.BlockSpec((1, tq, BRANCH_WIDTH), lambda bi, i: (bi, i, 0)),
                  pl.BlockSpec((1, t, 2 * BRANCH_WIDTH), lambda bi, i: (bi, 0, 0)),
                  _full((tk, tk))],
        out_specs=pl.BlockSpec((1, tq, BRANCH_WIDTH), lambda bi, i: (bi, i, 0)),
        out_shape=jax.ShapeDtypeStruct((b, t, BRANCH_WIDTH), F32),
        scratch_shapes=[pltpu.VMEM((t // tk, tk, BRANCH_WIDTH), BF16),
                        pltpu.VMEM((t // tk, BRANCH_WIDTH, tk), BF16)],
        compiler_params=_params("parallel", "arbitrary"),
        name="sb_prompt",
    )(q, kv, _tri(tk, upper=True))


def _sb_sample_kernel(pt_ref, q_ref, new_ref, tri_ref, *rest, n_pages, nd, nb):
    pages, (o_ref, newt_ref) = rest[:nb * n_pages], rest[nb * n_pages:]

    @pl.when(pl.program_id(0) == 0)
    def _():
        newt_ref[...] = jnp.zeros_like(newt_ref)

    rows = 4 * nd
    row = lax.broadcasted_iota(jnp.int32, (rows, 1), 0)
    lane = lax.broadcasted_iota(jnp.int32, (1, BRANCH_WIDTH), 1)
    head_mask = (_div(lane, HEAD_DIM) == _div(row, nd)).astype(F32)
    tri = tri_ref[...]
    key = lax.broadcasted_iota(jnp.int32, (1, newt_ref.shape[1]), 1)
    kv_half = lambda x, v: x[v * BRANCH_WIDTH:(v + 1) * BRANCH_WIDTH].astype(BF16)
    qms, tiles = [], []
    for s in range(nb):
        newt_ref[s, 0:nd, :] = new_ref[s]
        qms.append((q_ref[s] * SCALE * head_mask).astype(BF16))
        new_t = newt_ref[s].T
        seq_tiles = [(functools.partial(kv_half, new_t), key < _mod(row, nd))]
        for p in reversed(range(n_pages)):
            page_ref = pages[s * n_pages + p]
            seq_tiles.append((lambda v, r=page_ref: kv_half(r, v), None))
        tiles.append(seq_tiles)
    logs = [[_sb_logs(_dot(qm, get(0))) for get, _ in seq] for qm, seq in zip(qms, tiles)]
    keeps = [[lk if valid is None else jnp.where(valid, lk, 0.0) for (_, lk), (_, valid) in zip(lg, seq)]
             for lg, seq in zip(logs, tiles)]
    between = [[_split_dot(lk, tri) for lk in ks] for ks in keeps]
    weights = []
    for lg, ks, bt, seq in zip(logs, keeps, between, tiles):
        c = jnp.zeros((rows, 1), F32)
        ws = []
        for (lb, _), lk, b, (_, valid) in zip(lg, ks, bt, seq):
            w = jnp.exp(lb + b + c)
            ws.append((w if valid is None else jnp.where(valid, w, 0.0)).astype(BF16))
            c = c + jnp.sum(lk, axis=-1, keepdims=True)
        weights.append(ws)
    for s in range(nb):
        acc = sum(_dot_nt(w, get(1)) for w, (get, _) in zip(weights[s], tiles[s]))
        m = acc * head_mask
        hs = m[0:8] + m[8:16]
        o_ref[s] = hs + pltpu.roll(hs, 4, 0)


def _page_specs(cache, layer, n_pages, nb):
    width, page = cache.shape[2:]
    return [pl.BlockSpec((None, None, width, page), lambda b, pt, s=s, p=p: (layer, pt[b * nb + s, p], 0, 0))
            for s in range(nb) for p in range(n_pages)]


def _sb_sample(page_table, q16, new, cache, layer):
    ns, n_pages = page_table.shape
    nd = new.shape[1]
    assert nd == 4 and q16.shape[1] == 16
    width, page = cache.shape[2:]
    nb = math.gcd(ns, SAMPLE_SEQS_PER_STEP)
    grid_spec = pltpu.PrefetchScalarGridSpec(
        num_scalar_prefetch=1,
        grid=(ns // nb,),
        in_specs=[pl.BlockSpec((nb, 16, BRANCH_WIDTH), lambda b, pt: (b, 0, 0)),
                  pl.BlockSpec((nb, nd, width), lambda b, pt: (b, 0, 0)),
                  pl.BlockSpec((page, page), lambda b, pt: (0, 0))] + _page_specs(cache, layer, n_pages, nb),
        out_specs=pl.BlockSpec((nb, 8, BRANCH_WIDTH), lambda b, pt: (b, 0, 0)),
        scratch_shapes=[pltpu.VMEM((nb, page, width), F32)],
    )
    out = pl.pallas_call(
        functools.partial(_sb_sample_kernel, n_pages=n_pages, nd=nd, nb=nb),
        grid_spec=grid_spec,
        out_shape=jax.ShapeDtypeStruct((ns, 8, BRANCH_WIDTH), F32),
        compiler_params=_params("arbitrary"),
        name="sb_sample",
    )(page_table, q16, new, _tri(page, upper=False), *([cache] * (nb * n_pages)))
    return out[:, :nd]


def _compress_math(x_ref, nc, pitch, pe_ref, w_ref, kg, bd1):
    outs = []
    for kv in range(2):
        acc = jnp.zeros((nc, NSA_KV_WIDTH), F32)
        for l in range(CMP_BLOCK):
            xl = x_ref[kv, pl.ds(l, nc, stride=pitch), :] + pe_ref[kv, l:l + 1, :]
            acc = acc + _dot(xl.astype(BF16), w_ref[kv, l])
        outs.append(acc)
    return _head_rms(outs[0], bd1, kg), outs[1]


def _compress_prompt_kernel(x_ref, pe_ref, w_ref, kg_ref, bd_ref, o_ref, xs_ref):
    xs_ref[0] = x_ref[0, :, :NSA_KV_WIDTH]
    xs_ref[1] = x_ref[0, :, NSA_KV_WIDTH:]
    k, v = _compress_math(xs_ref, o_ref.shape[1], CMP_BLOCK, pe_ref, w_ref, kg_ref[...], bd_ref[...])
    o_ref[0, :, :NSA_KV_WIDTH] = k
    o_ref[0, :, NSA_KV_WIDTH:] = v


def _compress_prompt(x, pe, w, kg, bd1, layer):
    b, t, _ = x.shape
    nc = t // CMP_BLOCK
    return pl.pallas_call(
        _compress_prompt_kernel,
        grid=(b,),
        in_specs=[pl.BlockSpec((1, t, 2 * NSA_KV_WIDTH), lambda i: (i, 0, 0)), _layer(pe, layer),
                  _layer(w, layer), _layer(kg, layer), _full(bd1.shape)],
        out_specs=pl.BlockSpec((1, nc, 2 * NSA_KV_WIDTH), lambda i: (i, 0, 0)),
        out_shape=jax.ShapeDtypeStruct((b, nc, 2 * NSA_KV_WIDTH), F32),
        scratch_shapes=[pltpu.VMEM((2, t, NSA_KV_WIDTH), F32)],
        compiler_params=_params("parallel"),
        name="compress_prompt",
    )(x, pe, w, kg, bd1)


def _compress_sample_kernel(pt_ref, pe_ref, w_ref, kg_ref, bd_ref, *rest, n_pages, nb):
    pages, (o_ref, x_ref) = rest[:nb * n_pages], rest[nb * n_pages:]
    per_page = pages[0].shape[1] // CMP_BLOCK
    nc = o_ref.shape[1]
    for s in range(nb):
        for p in range(n_pages):
            for kv in range(2):
                rows = pages[s * n_pages + p][kv * NSA_KV_WIDTH:(kv + 1) * NSA_KV_WIDTH, :].T
                for q in range(per_page):
                    n = s * nc + p * per_page + q
                    x_ref[kv, pl.ds(n * CMP_PITCH, CMP_BLOCK), :] = rows[q * CMP_BLOCK:(q + 1) * CMP_BLOCK]
    k, v = _compress_math(x_ref, nb * nc, CMP_PITCH, pe_ref, w_ref, kg_ref[...], bd_ref[...])
    for s in range(nb):
        o_ref[s, :, :NSA_KV_WIDTH] = k[s * nc:(s + 1) * nc]
        o_ref[s, :, NSA_KV_WIDTH:] = v[s * nc:(s + 1) * nc]


def _compress_sample(page_table, cache, layer, pe, w, kg, bd1):
    ns, n_pages = page_table.shape
    width, page = cache.shape[2:]
    nc = n_pages * page // CMP_BLOCK
    nb = math.gcd(ns, SAMPLE_SEQS_PER_STEP)
    const = lambda shape: pl.BlockSpec(shape, lambda b, pt, n=len(shape): (0,) * n)
    grid_spec = pltpu.PrefetchScalarGridSpec(
        num_scalar_prefetch=1,
        grid=(ns // nb,),
        in_specs=[_layer(pe, layer), _layer(w, layer), _layer(kg, layer), const(bd1.shape)]
        + _page_specs(cache, layer, n_pages, nb),
        out_specs=pl.BlockSpec((nb, nc, 2 * NSA_KV_WIDTH), lambda b, pt: (b, 0, 0)),
        scratch_shapes=[pltpu.VMEM((2, -(-nb * nc * CMP_PITCH // SUBLANES) * SUBLANES, NSA_KV_WIDTH), F32)],
    )
    return pl.pallas_call(
        functools.partial(_compress_sample_kernel, n_pages=n_pages, nb=nb),
        grid_spec=grid_spec,
        out_shape=jax.ShapeDtypeStruct((ns, nc, 2 * NSA_KV_WIDTH), F32),
        compiler_params=_params("arbitrary"),
        name="compress_sample",
    )(page_table, pe, w, kg, bd1, *([cache] * (nb * n_pages)))


def _flash_t(q_ts, k_ref, vt_ref, lo, hi, tk, mask_fn):
    tq = q_ts[0].shape[1]

    def body(j, carry):
        k, vt = k_ref[j], vt_ref[j]
        masks = mask_fn(j, j * tk + lax.broadcasted_iota(jnp.int32, (tk, 1), 0))
        ss = [jnp.where(masks[ci // 2], _dot(k, q), NEG) for ci, q in enumerate(q_ts)]
        m2s = [jnp.maximum(m, jnp.max(s, axis=0, keepdims=True)) for s, (m, _, _) in zip(ss, carry)]
        ps = [jnp.exp(s - m2) for s, m2 in zip(ss, m2s)]
        pvs = [_dot(vt[(ci // 2) * HEAD_DIM:(ci // 2 + 1) * HEAD_DIM], p.astype(BF16)) for ci, p in enumerate(ps)]
        out = []
        for (m, l, acc), m2, p, pv in zip(carry, m2s, ps, pvs):
            a = jnp.exp(m - m2)
            out.append((m2, a * l + jnp.sum(p, axis=0, keepdims=True), a * acc + pv))
        return tuple(out)

    init = tuple((jnp.full((1, tq), NEG, F32), jnp.zeros((1, tq), F32), jnp.zeros((HEAD_DIM, tq), F32))
                 for _ in q_ts)
    return [acc / l for _, l, acc in lax.fori_loop(lo, hi, body, init)]


def _rank_select(score, n_sel):
    nb = score.shape[0]
    row = lax.broadcasted_iota(jnp.int32, (nb, 1), 0)
    ahead = jnp.zeros(score.shape, F32)
    for m in range(nb):
        sm = score[m:m + 1, :]
        ahead = ahead + jnp.where((sm > score) | ((sm == score) & (row > m)), 1.0, 0.0)
    return (ahead < n_sel).astype(F32)


def _block_scores(pair, tvec):
    n = lax.broadcasted_iota(jnp.int32, (pair.shape[0], 1), 0)
    cur = _div(tvec, SEL_BLOCK)
    forced = (n == 0) | (n == cur) | (n == cur - 1)
    return jnp.where(forced, 1e30, jnp.where(n * SEL_BLOCK <= tvec, pair, -1.0))


def _nsa_prompt_kernel(qn_ref, qr_ref, gate_ref, kc_ref, sel_ref, win_ref, o_ref,
                       selk_ref, selvt_ref, wink_ref, winvt_ref, kck_ref, kcvt_ref, imp_ref, sel8_ref,
                       *, tq, tk, n_sel, ns):
    i = pl.program_id(1)
    n_all = selk_ref.shape[0]
    nc = kc_ref.shape[1]
    blocks_per_tile = tk // SEL_BLOCK

    @pl.when(i == 0)
    def _():
        for src, k_ref, vt_ref in ((sel_ref, selk_ref, selvt_ref), (win_ref, wink_ref, winvt_ref)):
            for j in range(n_all):
                tile = src[0, j * tk:(j + 1) * tk, :]
                k_ref[j] = tile[:, :NSA_KV_WIDTH].astype(BF16)
                vt_ref[j] = tile[:, NSA_KV_WIDTH:].T.astype(BF16)
        kc = kc_ref[0]
        pad = jnp.zeros((LANES - nc, NSA_KV_WIDTH), F32)
        kck_ref[...] = jnp.concatenate([kc[:, :NSA_KV_WIDTH], pad], axis=0).astype(BF16)
        kcvt_ref[...] = jnp.concatenate([kc[:, NSA_KV_WIDTH:], pad], axis=0).T.astype(BF16)
        sel8_ref[...] = jnp.zeros_like(sel8_ref)

    t0 = i * tq
    feat = lax.broadcasted_iota(jnp.int32, (LANES, 1), 0)
    tlane = lax.broadcasted_iota(jnp.int32, (1, tq), 1)
    tvec = t0 + jnp.concatenate([tlane] * 4, axis=1)
    tvec2 = t0 + jnp.concatenate([tlane] * 2, axis=1)

    def stack_t(q):
        q_feat = q.T * SCALE
        parts = [jnp.where(_div(feat, HEAD_DIM) == g, q_feat[j * LANES:(j + 1) * LANES], 0.0)
                 for g in (0, 1) for j in (0, 1)]
        return jnp.concatenate(parts, axis=1).astype(BF16)

    qn_t, qr_t = stack_t(qn_ref[0]), stack_t(qr_ref[0])

    c = lax.broadcasted_iota(jnp.int32, (LANES, 1), 0)
    cvalid = (((c + 1) * CMP_BLOCK - 1) <= tvec) & (c < nc)
    s = jnp.where(cvalid, _dot(kck_ref[...], qn_t), NEG)
    e = jnp.exp(s - jnp.max(s, axis=0, keepdims=True))
    any_valid = jnp.max(cvalid.astype(F32), axis=0, keepdims=True)
    p = e / jnp.sum(e, axis=0, keepdims=True) * any_valid
    o_cmp = _dot(kcvt_ref[...], p.astype(BF16))

    imp_ref[0] = p[:, 0:tq] + p[:, tq:2 * tq]
    imp_ref[1] = p[:, 2 * tq:3 * tq] + p[:, 3 * tq:4 * tq]
    nb = sel8_ref.shape[0] * blocks_per_tile
    pair = jnp.concatenate([imp_ref[g, pl.ds(0, nb, stride=2), :] + imp_ref[g, pl.ds(1, nb, stride=2), :]
                            for g in (0, 1)], axis=1)
    sel = _rank_select(_block_scores(pair, tvec2), n_sel)
    for j in range(n_all):
        sel8_ref[j, 0:blocks_per_tile, :] = sel[j * blocks_per_tile:(j + 1) * blocks_per_tile, :]

    hi = (t0 + tq - 1) // tk + 1
    krow = lax.broadcasted_iota(jnp.int32, (tk, 1), 0)

    tq_vec = t0 + tlane

    def sel_mask(j, kpos):
        s8 = sel8_ref[j]
        chosen = s8[0:1, :]
        for bi in range(1, blocks_per_tile):
            chosen = jnp.where(krow >= bi * SEL_BLOCK, s8[bi:bi + 1, :], chosen)
        causal = kpos <= tq_vec
        return [(chosen[:, g * tq:(g + 1) * tq] > 0.5) & causal for g in (0, 1)]

    def win_mask(j, kpos):
        inside = (kpos <= tq_vec) & (kpos > tq_vec - WINDOW)
        return [inside, inside]

    qr_ts = [qr_t[:, ci * tq:(ci + 1) * tq] for ci in range(4)]
    o_sel = _flash_t(qr_ts, selk_ref, selvt_ref, 0, hi, tk, sel_mask)
    lo = jnp.maximum(t0 - WINDOW + 1, 0) // tk
    o_win = _flash_t(qr_ts, wink_ref, winvt_ref, lo, hi, tk, win_mask)

    gate_t = gate_ref[0].T
    parts = []
    for j in (0, 1):
        for g in (0, 1):
            ci = 2 * g + j
            o_c = o_cmp[g * HEAD_DIM:(g + 1) * HEAD_DIM, ci * tq:(ci + 1) * tq]
            y = 0.0
            for br, o in enumerate((o_c, o_sel[ci], o_win[ci])):
                r0 = br * BRANCH_WIDTH + j * LANES + g * HEAD_DIM
                y = y + gate_t[r0:r0 + HEAD_DIM] * o
            parts.append(y)
    o_ref[0] = jnp.concatenate(parts, axis=0).T


def _nsa_prompt(qn, qr, gate, kc, sel, win):
    b, t, _ = qn.shape
    tq, tk = min(ATT_TQ, t), min(NSA_TK, t)
    assert tq == LANES and tk // SEL_BLOCK <= SUBLANES
    assert tk % SEL_BLOCK == 0
    ns = t // SEL_BLOCK
    n_sel = min(N_SELECT, ns)
    n_tiles = t // tk
    qspec = lambda w: pl.BlockSpec((1, tq, w), lambda bi, i: (bi, i, 0))
    seq = lambda n, w: pl.BlockSpec((1, n, w), lambda bi, i: (bi, 0, 0))
    tiles = lambda r, c: pltpu.VMEM((n_tiles, r, c), BF16)
    return pl.pallas_call(
        functools.partial(_nsa_prompt_kernel, tq=tq, tk=tk, n_sel=n_sel, ns=ns),
        grid=(b, t // tq),
        in_specs=[qspec(256), qspec(256), qspec(768), seq(kc.shape[1], 256), seq(t, 256), seq(t, 256)],
        out_specs=qspec(256),
        out_shape=jax.ShapeDtypeStruct((b, t, BRANCH_WIDTH), F32),
        scratch_shapes=[tiles(tk, NSA_KV_WIDTH), tiles(NSA_KV_WIDTH, tk), tiles(tk, NSA_KV_WIDTH),
                        tiles(NSA_KV_WIDTH, tk),
                        pltpu.VMEM((LANES, NSA_KV_WIDTH), BF16), pltpu.VMEM((NSA_KV_WIDTH, LANES), BF16),
                        pltpu.VMEM((2, LANES, tq), F32), pltpu.VMEM((n_tiles, SUBLANES, 2 * tq), F32)],
        compiler_params=_params("parallel", "arbitrary"),
        name="nsa_prompt",
    )(qn, qr, gate, kc, sel, win)


def _compressed_attend(qn, kc, tvec):
    nc = kc.shape[0]
    pad = jnp.zeros((LANES - nc, NSA_KV_WIDTH), F32)
    kck = jnp.concatenate([kc[:, :NSA_KV_WIDTH], pad], axis=0).astype(BF16)
    kcv = jnp.concatenate([kc[:, NSA_KV_WIDTH:], pad], axis=0).astype(BF16)
    lane = lax.broadcasted_iota(jnp.int32, (1, LANES), 1)
    valid = (((lane + 1) * CMP_BLOCK - 1) <= tvec) & (lane < nc)
    s = jnp.where(valid, _dot_nt(qn, kck), NEG)
    e = jnp.exp(s - jnp.max(s, axis=-1, keepdims=True))
    any_valid = jnp.max(valid.astype(F32), axis=-1, keepdims=True)
    p = e / jnp.sum(e, axis=-1, keepdims=True) * any_valid
    return p, _dot(p.astype(BF16), kcv)


def _softmax_scores(raw, masks, vts):
    scores = [jnp.where(msk, s, NEG) for s, msk in zip(raw, masks)]
    m = functools.reduce(jnp.maximum, [jnp.max(s, axis=-1, keepdims=True) for s in scores])
    es = [jnp.exp(s - m) for s in scores]
    l = sum(jnp.sum(e, axis=-1, keepdims=True) for e in es)
    o = sum(_dot_nt(e.astype(BF16), vt) for e, vt in zip(es, vts))
    return o / l


def _nsa_sample_kernel(pt_ref, qn_ref, qr_ref, gate_ref, kc_ref, seln_ref, win_ref, winn_ref, e_ref, *rest,
                       n_pages, nd, past, n_sel, nb):
    pages, (o_ref, wino_ref, newt_ref, impt_ref) = rest[:nb * n_pages], rest[nb * n_pages + 1:]

    @pl.when(pl.program_id(0) == 0)
    def _():
        newt_ref[...] = jnp.zeros_like(newt_ref)

    rows = 4 * nd
    row = lax.broadcasted_iota(jnp.int32, (rows, 1), 0)
    lane = lax.broadcasted_iota(jnp.int32, (1, LANES), 1)
    group_mask = (_div(lane, HEAD_DIM) == _div(row, 2 * nd)).astype(F32)
    tau = _mod(row, nd)
    page = pages[0].shape[1]
    wb = win_ref.shape[2]
    assert page == LANES

    k_half = lambda x: x[:NSA_KV_WIDTH].astype(BF16)
    v_half = lambda x: x[NSA_KV_WIDTH:].astype(BF16)

    ps, o_cmps, o_wins, sel_raw, sel_vts = [], [], [], [], []
    for s in range(nb):
        newt_ref[s, 0, 0:nd, :] = seln_ref[s]
        newt_ref[s, 1, 0:nd, :] = winn_ref[s]
        qn = (qn_ref[s] * SCALE * group_mask).astype(BF16)
        qr = (qr_ref[s] * SCALE * group_mask).astype(BF16)
        p, o_cmp = _compressed_attend(qn, kc_ref[s], past + tau)
        ps.append(p)
        o_cmps.append(o_cmp)

        new_win_t = newt_ref[s, 1].T
        win_tiles = [win_ref[s, :, wi * page:(wi + 1) * page] for wi in range(wb // page)] + [new_win_t]
        masks = [(wi * page + lane > tau + (wb - WINDOW)) & (wi * page + lane >= wb - past)
                 for wi in range(wb // page)] + [lane <= tau]
        o_wins.append(_softmax_scores([_dot(qr, k_half(wt)) for wt in win_tiles], masks,
                                      [v_half(wt) for wt in win_tiles]))
        rolled = pltpu.roll(win_ref[s], wb - nd, 1)
        wino_ref[s, :, 0:wb - page] = rolled[:, 0:wb - page]
        wino_ref[s, :, wb - page:wb] = jnp.where(lane >= page - nd, pltpu.roll(new_win_t, page - nd, 1),
                                                 rolled[:, wb - page:wb])

        sel_tiles = [pages[s * n_pages + pi] for pi in range(n_pages)] + [newt_ref[s, 0].T]
        sel_raw.append([_dot(qr, k_half(st)) for st in sel_tiles])
        sel_vts.append(sel_tiles)

    p_all = jnp.concatenate(ps, axis=0)
    imp = p_all + pltpu.roll(p_all, nd, 0)
    impt_ref[...] = jnp.concatenate([imp, jnp.zeros((LANES - nb * rows, LANES), F32)], axis=0).T
    nbk = -(-((past + nd - 1) // SEL_BLOCK + 1) // SUBLANES) * SUBLANES
    pair = impt_ref[pl.ds(0, nbk, stride=2), :] + impt_ref[pl.ds(1, nbk, stride=2), :]
    sel_t = _rank_select(_block_scores(pair, past + _mod(lane, nd)), n_sel)
    sel_rn = jnp.concatenate([sel_t, jnp.zeros((LANES - nbk, LANES), F32)], axis=0).T[:nb * rows]
    row_all = lax.broadcasted_iota(jnp.int32, (nb * rows, 1), 0)
    sel4_all = jnp.where(_mod(_div(row_all, nd), 2) == 1, sel_rn,
                         pltpu.roll(sel_rn, nb * rows - nd, 0)).astype(BF16)

    for s in range(nb):
        sel4 = sel4_all[s * rows:(s + 1) * rows]
        masks = [_dot(sel4, e_ref[pi]) > 0.5 for pi in range(n_pages)]
        masks.append((_dot(sel4, e_ref[n_pages]) > 0.5) & (lane <= tau))
        o_sel = _softmax_scores(sel_raw[s], masks, [v_half(st) for st in sel_vts[s]])
        y = (gate_ref[s, 0] * o_cmps[s] + gate_ref[s, 1] * o_sel + gate_ref[s, 2] * o_wins[s]) * group_mask
        o_ref[s] = y[0:2 * nd] + y[2 * nd:4 * nd]


def _expand_matrix(n_tiles, tk):
    n = np.arange(LANES)[None, :, None]
    key = (np.arange(n_tiles)[:, None, None] * tk + np.arange(tk)[None, None, :])
    return jnp.asarray(n == key // SEL_BLOCK, BF16)


def _nsa_sample(page_table, qn16, qr16, gate16, kc, sel_new, win_state, win_new, cache_sel, layer, past, win_acc):
    ns, n_pages = page_table.shape
    nd = sel_new.shape[1]
    assert nd == 4
    width, page = cache_sel.shape[2:]
    wb = win_state.shape[3]
    assert wb % page == 0 and page == LANES
    n_sel = min(N_SELECT, -(-(past + nd) // SEL_BLOCK))
    e = _expand_matrix(n_pages + 1, page)
    nb = math.gcd(ns, SAMPLE_SEQS_PER_STEP)
    per_seq = lambda *s: pl.BlockSpec((nb,) + s, lambda b, pt, n=len(s): (b,) + (0,) * n)
    win_spec = pl.BlockSpec((None, nb, 256, wb), lambda b, pt: (layer, b, 0, 0))
    n_in = 9 + nb * n_pages
    grid_spec = pltpu.PrefetchScalarGridSpec(
        num_scalar_prefetch=1,
        grid=(ns // nb,),
        in_specs=[per_seq(16, LANES), per_seq(16, LANES), per_seq(3, 16, LANES), per_seq(kc.shape[1], 256),
                  per_seq(nd, 256), win_spec, per_seq(nd, 256),
                  pl.BlockSpec(e.shape, lambda b, pt: (0, 0, 0))] + _page_specs(cache_sel, layer, n_pages, nb)
        + [pl.BlockSpec(memory_space=pl.ANY)],
        out_specs=[per_seq(2 * nd, LANES), win_spec],
        scratch_shapes=[pltpu.VMEM((nb, 2, page, 256), F32), pltpu.VMEM((LANES, LANES), F32)],
    )
    return pl.pallas_call(
        functools.partial(_nsa_sample_kernel, n_pages=n_pages, nd=nd, past=past, n_sel=n_sel, nb=nb),
        grid_spec=grid_spec,
        out_shape=[jax.ShapeDtypeStruct((ns, 2 * nd, LANES), F32), jax.ShapeDtypeStruct(win_acc.shape, F32)],
        input_output_aliases={n_in: 1},
        compiler_params=_params("arbitrary"),
        name="nsa_sample",
    )(page_table, qn16, qr16, gate16, kc, sel_new, win_state, win_new, e, *([cache_sel] * (nb * n_pages)), win_acc)


_SC_HALO, _CC_HALO = 8, 32


def _conv_kernel(u_ref, scb_ref, a_ref, uprev_ref, aprev_ref, stsc_ref, stcc_ref,
                 scw_ref, ccw_ref, ccb_ref, ccg_ref, ccbeta_ref,
                 yc_ref, yd_ref, nsc_ref, ncc_ref, fu_ref, fa_ref, *, tc):
    c = pl.program_id(1)

    @pl.when(c == 0)
    def _():
        fu_ref[:, _SC_HALO - (SC_CONV_LEN - 1):_SC_HALO, :] = stsc_ref[...]
        fa_ref[:, _CC_HALO - (CC_CONV_LEN - 1):_CC_HALO, :] = stcc_ref[...]

    @pl.when(c > 0)
    def _():
        fu_ref[:, 0:_SC_HALO, :] = uprev_ref[...]
        fa_ref[:, 0:_CC_HALO, :] = aprev_ref[...]

    fu_ref[:, _SC_HALO:, :] = u_ref[...]
    fa_ref[:, _CC_HALO:, :] = a_ref[...]
    sub = min(CONV_SUB, tc)
    for s in range(tc // sub):
        r0 = s * sub
        acc = 0.0
        for i in range(SC_CONV_LEN):
            o = r0 + i + _SC_HALO - (SC_CONV_LEN - 1)
            acc = acc + scw_ref[i:i + 1, :] * fu_ref[:, o:o + sub, :]
        yc_ref[:, r0:r0 + sub, :] = scb_ref[:, r0:r0 + sub, :] * acc
        acc = 0.0
        for i in range(CC_CONV_LEN):
            o = r0 + i + _CC_HALO - (CC_CONV_LEN - 1)
            acc = acc + ccw_ref[i:i + 1, :] * fa_ref[:, o:o + sub, :]
        z = acc + ccb_ref[...]
        mu = jnp.mean(z, axis=-1, keepdims=True)
        var = jnp.mean(jnp.square(z - mu), axis=-1, keepdims=True)
        ln = (z - mu) * lax.rsqrt(var + NORM_EPS) * ccg_ref[...] + ccbeta_ref[...]
        yd_ref[:, r0:r0 + sub, :] = ln * jax.nn.sigmoid(ln)
    nsc_ref[...] = fu_ref[:, tc + _SC_HALO - (SC_CONV_LEN - 1):tc + _SC_HALO, :]
    ncc_ref[...] = fa_ref[:, tc + _CC_HALO - (CC_CONV_LEN - 1):tc + _CC_HALO, :]


def _conv(u, scb, a, st_sc, st_cc, scw, ccw, ccb, ccg, ccbeta, nb):
    b, t, w = u.shape
    tc = min(CONV_TC, t)
    chunked = t > tc
    if chunked:
        uprev, aprev = u, a
        prev_u = pl.BlockSpec((nb, _SC_HALO, w), lambda bi, c: (bi, jnp.maximum(c * (tc // _SC_HALO) - 1, 0), 0))
        prev_a = pl.BlockSpec((nb, _CC_HALO, w), lambda bi, c: (bi, jnp.maximum(c * (tc // _CC_HALO) - 1, 0), 0))
    else:
        uprev, aprev = jnp.zeros((b, _SC_HALO, w), F32), jnp.zeros((b, _CC_HALO, w), F32)
        prev_u = pl.BlockSpec((nb, _SC_HALO, w), lambda bi, c: (bi, 0, 0))
        prev_a = pl.BlockSpec((nb, _CC_HALO, w), lambda bi, c: (bi, 0, 0))
    cur = pl.BlockSpec((nb, tc, w), lambda bi, c: (bi, c, 0))
    state = lambda n: pl.BlockSpec((nb, n, w), lambda bi, c: (bi, 0, 0))
    wspec = lambda arr: pl.BlockSpec(arr.shape, lambda bi, c: (0, 0))
    return pl.pallas_call(
        functools.partial(_conv_kernel, tc=tc),
        grid=(b // nb, t // tc),
        in_specs=[cur, cur, cur, prev_u, prev_a, state(SC_CONV_LEN - 1), state(CC_CONV_LEN - 1),
                  wspec(scw), wspec(ccw), wspec(ccb), wspec(ccg), wspec(ccbeta)],
        out_specs=[cur, cur, state(SC_CONV_LEN - 1), state(CC_CONV_LEN - 1)],
        out_shape=[jax.ShapeDtypeStruct((b, t, w), F32), jax.ShapeDtypeStruct((b, t, w), F32),
                   jax.ShapeDtypeStruct((b, SC_CONV_LEN - 1, w), F32),
                   jax.ShapeDtypeStruct((b, CC_CONV_LEN - 1, w), F32)],
        scratch_shapes=[pltpu.VMEM((nb, _SC_HALO + tc, w), F32), pltpu.VMEM((nb, _CC_HALO + tc, w), F32)],
        compiler_params=_params("parallel", "arbitrary"),
        name="conv",
    )(u, scb, a, uprev, aprev, st_sc, st_cc, scw, ccw, ccb, ccg, ccbeta)


def _merge_kernel(x_ref, ya_ref, yb_ref, yc_ref, yd_ref, g_ref, wg_ref, bg_ref, wb_ref, wo_ref, o_ref):
    x = x_ref[...]
    d = x.shape[-1]
    h = _rms(x, g_ref[...]).astype(BF16)
    merged = 0.0
    for n, y_ref in enumerate((ya_ref, yb_ref, yc_ref, yd_ref)):
        gate = jax.nn.sigmoid(_dot(h, wg_ref[:, n * d:(n + 1) * d]) + bg_ref[:, n * d:(n + 1) * d])
        merged = merged + gate * _dot(y_ref[...].astype(BF16), wb_ref[n])
    o_ref[...] = x + _dot(merged.astype(BF16), wo_ref[...])


def _merge(x, ya, yb, yc, yd, g, wg, bg, wb, wo, layer):
    n, d = x.shape
    tm = min(TOKEN_TILE, n)
    tok = lambda w: pl.BlockSpec((tm, w), lambda i: (i, 0))
    return pl.pallas_call(
        _merge_kernel,
        grid=(n // tm,),
        in_specs=[tok(d)] + [tok(BRANCH_WIDTH)] * 4 + [_layer(a, layer) for a in (g, wg, bg, wb, wo)],
        out_specs=tok(d),
        out_shape=jax.ShapeDtypeStruct((n, d), F32),
        compiler_params=_params("parallel"),
        name="merge",
    )(x, ya, yb, yc, yd, g, wg, bg, wb, wo)


def _regroup_w_in(w_in):
    splits = (BRANCH_WIDTH,) * 4 + (NSA_KV_WIDTH,) * 6 + (12,) + (BRANCH_WIDTH,) * 5
    offs = np.concatenate([[0], np.cumsum(splits)])
    o_nq, o_ck, o_ng, o_scb = offs[3], offs[4], offs[10], offs[11]
    heads_jg = [g * 2 + j for j in (0, 1) for g in (0, 1)]
    jg_heads = np.concatenate([np.arange(h * HEAD_DIM, (h + 1) * HEAD_DIM) for h in heads_jg])
    nq = [w_in[..., o_nq + h * HEAD_DIM: o_nq + (h + 1) * HEAD_DIM] for h in heads_jg]
    gate_cols = np.array([o_ng + g * 6 + j * 3 + br for br in range(3) for j in (0, 1) for g in (0, 1)])
    gates = jnp.repeat(w_in[..., gate_cols], HEAD_DIM, axis=-1)
    out = jnp.concatenate([w_in[..., :o_nq]] + nq + [w_in[..., o_ck:o_ng], gates, w_in[..., o_scb:]], axis=-1)
    assert out.shape[-1] == _P_END
    return out.astype(BF16), jg_heads


def _rope_tables(pos):
    half = HEAD_DIM // 2
    inv_freq = jnp.exp(-math.log(ROPE_THETA) * jnp.arange(half, dtype=F32) / half)
    ang = pos.astype(F32)[:, None] * inv_freq[None, :]
    cos, sin = jnp.cos(ang), jnp.sin(ang)
    reps = BRANCH_WIDTH // HEAD_DIM
    return jnp.tile(jnp.concatenate([cos, cos], -1), (1, reps)), jnp.tile(jnp.concatenate([-sin, sin], -1), (1, reps))


def _rows16(a, nd):
    ns = a.shape[0]
    a = a.reshape(ns, nd, 2, LANES).transpose(0, 2, 1, 3)
    return jnp.broadcast_to(a[:, None], (ns, 2, 2, nd, LANES)).reshape(ns, 4 * nd, LANES)


def _feature_major(cache):
    l, p, rows = cache.shape[:3]
    return cache.transpose(0, 1, 3, 4, 5, 2).reshape(l, p, -1, rows)


def kernel(x_prompt, x_sample, cache_sb_kv, cache_cmp_kv, cache_sel_kv, state_win_kv, state_conv_sc,
           state_conv_cc, page_table, norm_ffn1, w_ffn1_gu, w_ffn1_down, norm_mix, w_in, nsa_q_gain,
           nsa_k_gain, nsa_cmp_pe, nsa_cmp_w, sc_conv_w, cc_conv_w, cc_conv_b, cc_norm_g, cc_norm_b,
           w_branch, w_gate, b_gate, w_out, norm_ffn2, w_ffn2_gu, w_ffn2_down):
    b, t, d = x_prompt.shape
    ns, nd, _ = x_sample.shape
    depth, n_phys, page = cache_sb_kv.shape[:3]
    n_pages = page_table.shape[1]
    past = n_pages * page
    assert w_ffn1_down.shape[1] % FFN_CHUNK == 0
    wb = state_win_kv.shape[2]

    wgu1, wgu2 = w_ffn1_gu.astype(BF16), w_ffn2_gu.astype(BF16)
    wd1, wd2 = w_ffn1_down.astype(BF16), w_ffn2_down.astype(BF16)
    w_in_r, jg_heads = _regroup_w_in(w_in)
    w_gate_b = w_gate.astype(BF16)
    w_branch_b = w_branch.at[:, 1].set(w_branch[:, 1][:, jg_heads, :]).astype(BF16)
    w_out_b = w_out.astype(BF16)
    eye2 = jnp.eye(2, dtype=F32)
    w_cmp = jnp.einsum("Lklde,gG->LklgdGe", nsa_cmp_w, eye2).reshape(depth, 2, CMP_BLOCK, 128, 128)
    w_cmp = w_cmp.astype(BF16)
    pe_rows = jnp.tile(nsa_cmp_pe, (1, 1, 1, 2))
    row2 = lambda a: a.reshape(depth, 1, -1)
    n1, nm, n2, bg = row2(norm_ffn1), row2(norm_mix), row2(norm_ffn2), row2(b_gate)
    qg = jnp.tile(row2(nsa_q_gain), (1, 1, 4))
    kg = jnp.tile(row2(nsa_k_gain), (1, 1, 2))
    ccb, ccg, ccbeta = row2(cc_conv_b), row2(cc_norm_g), row2(cc_norm_b)
    hd = np.arange(BRANCH_WIDTH) // HEAD_DIM
    bd = jnp.asarray(hd[:, None] == hd[None, :], BF16)
    bd1 = bd[:LANES, :LANES]
    cos_p, sin_p = _rope_tables(jnp.arange(t, dtype=jnp.int32))
    cos_s, sin_s = _rope_tables(past + jnp.arange(ns * nd, dtype=jnp.int32) % nd)

    cache_sb = _feature_major(cache_sb_kv)
    cache_cmp = _feature_major(cache_cmp_kv)
    cache_sel = _feature_major(cache_sel_kv)
    win_state = _feature_major(state_win_kv)

    xp = x_prompt.reshape(b * t, d)
    xs = x_sample.reshape(ns * nd, d)
    zeros_sc = jnp.zeros((b, SC_CONV_LEN - 1, BRANCH_WIDTH), F32)
    zeros_cc = jnp.zeros((b, CC_CONV_LEN - 1, BRANCH_WIDTH), F32)
    tw = min(WINDOW, t)
    assert t % min(TOKEN_TILE, b * t) == 0 and tw in (t, min(TOKEN_TILE, b * t))
    stacked_p = tuple(jnp.zeros((depth, b, w_, t_), F32) for w_, t_ in ((512, t), (256, t), (256, t), (256, tw)))
    win_acc = jnp.zeros((depth, ns, 256, wb), F32)
    outs_p = [[] for _ in range(2)]
    outs_s = [[] for _ in range(5)]
    for l in range(depth):
        xp = _ffn(xp, n1, wgu1, wd1, l)
        xs = _ffn(xs, n1, wgu1, wd1, l)

        (sbq, sbkv, qn, qr, cmp_, sel, win, gate, scb, scu, cca), stacked_p = _proj(
            xp, nm, w_in_r, cos_p, sin_p, qg, kg, bd, l, stacked_p)
        seq = lambda a: a.reshape(b, t, a.shape[-1])
        ya = _sb_prompt(seq(sbq), seq(sbkv))
        kc = _compress_prompt(seq(cmp_), pe_rows, w_cmp, kg, bd1, l)
        yb = _nsa_prompt(seq(qn), seq(qr), seq(gate), kc, seq(sel), seq(win))
        yc, yd, nsc, ncc = _conv(seq(scu), seq(scb), seq(cca), zeros_sc, zeros_cc, sc_conv_w[l], cc_conv_w[l],
                                 ccb[l], ccg[l], ccbeta[l], nb=1)
        flat = lambda a: a.reshape(b * t, a.shape[-1])
        xp = _merge(xp, flat(ya), flat(yb), flat(yc), flat(yd), nm, w_gate_b, bg, w_branch_b, w_out_b, l)
        for lst, a in zip(outs_p, (nsc, ncc)):
            lst.append(a)

        (sbq, sbkv, qn, qr, cmp_, sel, win, gate, scb, scu, cca), _ = _proj(
            xs, nm, w_in_r, cos_s, sin_s, qg, kg, bd, l)
        seq = lambda a: a.reshape(ns, nd, a.shape[-1])
        q16 = jnp.tile(seq(sbq), (1, 4, 1))
        ya = _sb_sample(page_table, q16, seq(sbkv), cache_sb, l)
        kc = _compress_sample(page_table, cache_cmp, l, pe_rows, w_cmp, kg, bd1)
        gate16 = jnp.stack([_rows16(seq(gate)[:, :, br * 256:(br + 1) * 256], nd) for br in range(3)], axis=1)
        y8, win_acc = _nsa_sample(page_table, _rows16(seq(qn), nd), _rows16(seq(qr), nd), gate16, kc,
                                  seq(sel), win_state, seq(win), cache_sel, l, past, win_acc)
        yb = y8.reshape(ns, 2, nd, LANES).transpose(0, 2, 1, 3).reshape(ns, nd, BRANCH_WIDTH)
        yc, yd, nsc, ncc = _conv(seq(scu), seq(scb), seq(cca), state_conv_sc[l], state_conv_cc[l],
                                 sc_conv_w[l], cc_conv_w[l], ccb[l], ccg[l], ccbeta[l], nb=min(8, ns))
        flat = lambda a: a.reshape(ns * nd, a.shape[-1])
        xs = _merge(xs, flat(ya), flat(yb), flat(yc), flat(yd), nm, w_gate_b, bg, w_branch_b, w_out_b, l)
        for lst, a in zip(outs_s, (seq(sbkv), seq(cmp_), seq(sel), nsc, ncc)):
            lst.append(a)

        xp = _ffn(xp, n2, wgu2, wd2, l)
        xs = _ffn(xs, n2, wgu2, wd2, l)

    def kv5(a, heads):
        a = jnp.stack(a)
        return a.reshape(a.shape[:3] + (2, heads, HEAD_DIM))

    def rows_last(a, heads):
        return a.reshape(a.shape[:2] + (2, heads, HEAD_DIM, a.shape[3])).transpose(0, 1, 5, 2, 3, 4)

    sb_heads, nsa_kv_heads = BRANCH_WIDTH // HEAD_DIM, NSA_KV_WIDTH // HEAD_DIM
    return (xp.reshape(b, t, d), xs.reshape(ns, nd, d),
            rows_last(stacked_p[0], sb_heads), kv5(outs_s[0], sb_heads),
            rows_last(stacked_p[1], nsa_kv_heads), kv5(outs_s[1], nsa_kv_heads),
            rows_last(stacked_p[2], nsa_kv_heads), kv5(outs_s[2], nsa_kv_heads),
            rows_last(stacked_p[3], nsa_kv_heads), rows_last(win_acc, nsa_kv_heads),
            jnp.stack(outs_p[0]), jnp.stack(outs_s[3]), jnp.stack(outs_p[1]), jnp.stack(outs_s[4]))
```
